```python
import math
import jax, jax.numpy as jnp
from jax import lax
import numpy as np


D_MODEL = 2048
BATCH = 1
SEQ = 8192
DEPTH = 1

MEM_LEN = 256
EPS = 1e-6
NEG_INF = -1e30
ROPE_THETA = 500000.0
Q_BLOCK = 128
DA_HEADS = 8
DA_DK = 64
DA_DV = 2 * DA_DK
NSA_HEADS = 16
NSA_GROUPS = 4
NSA_HPG = NSA_HEADS // NSA_GROUPS
NSA_DK = 64
CMP_LEN = 32
CMP_STRIDE = 16
CMP_HIDDEN = 128
SEL_LEN = 64
SEL_TOPK = 16
WINDOW = 512
FORCE_BONUS = 100.0
ROPE_DIM = DA_DK // 4
X_HEADS = 4
X_DH = 128
PEER_HEADS = 8
PEER_NKEYS = 128
PEER_EXPERTS = PEER_NKEYS * PEER_NKEYS
PEER_DQ = 256
PEER_TOPK = 16
PEER_CHUNK = 128
SPLIT_SIZES = (DA_HEADS * 2 * DA_DK, DA_HEADS * 2 * DA_DK, DA_HEADS * DA_DV,
               NSA_HEADS * NSA_DK, 6 * NSA_GROUPS * NSA_DK, NSA_HEADS * 3, 2 * D_MODEL)
IN_COLS = sum(SPLIT_SIZES)
SPLIT_POINTS = [int(o) for o in np.cumsum(SPLIT_SIZES)[:-1]]

kernel_name = 'hybrid_diffattn_nsa_peer_block'


def rms_norm(x, g):
    xf = x.astype(jnp.float32)
    y = xf * lax.rsqrt(jnp.mean(xf * xf, axis=-1, keepdims=True) + EPS)
    return (y * g.astype(jnp.float32)).astype(x.dtype)


def rope_tables(positions, dtype):
    inv = ROPE_THETA ** (-jnp.arange(0, ROPE_DIM, 2, dtype=jnp.float32) / ROPE_DIM)
    ang = positions.astype(jnp.float32)[..., None] * inv
    return jnp.cos(ang).astype(dtype), jnp.sin(ang).astype(dtype)


def partial_rope(x, cos, sin):
    half = ROPE_DIM // 2
    shp = cos.shape[:2] + (1,) * (x.ndim - 3) + (half,)
    c = cos.reshape(shp)
    s = sin.reshape(shp)
    x1 = x[..., :half]
    x2 = x[..., half:ROPE_DIM]
    return jnp.concatenate([x1 * c - x2 * s, x2 * c + x1 * s, x[..., ROPE_DIM:]], axis=-1)


def masked_softmax(s, mask):
    s32 = jnp.where(mask, s.astype(jnp.float32), NEG_INF)
    return jax.nn.softmax(s32, axis=-1) * mask


def diff_attention(q, k, v, lam, lambda_init, subln_g):
    B, S = q.shape[:2]
    scale = DA_DK ** -0.5
    kpos = jnp.arange(S)

    def block(c):
        q0 = c * Q_BLOCK
        qb = lax.dynamic_slice_in_dim(q, q0, Q_BLOCK, axis=1)
        t = q0 + jnp.arange(Q_BLOCK)
        s = jnp.einsum('bqhmd,bkhmd->bhmqk', qb, k) * scale
        mask = kpos[None, :] <= t[:, None]
        p = masked_softmax(s, mask)
        a = p[:, :, 0] - lam * p[:, :, 1]
        return jnp.einsum('bhqk,bkhd->bqhd', a.astype(v.dtype), v)

    o = lax.map(block, jnp.arange(S // Q_BLOCK))
    o = jnp.moveaxis(o, 0, 1).reshape(B, S, DA_HEADS, DA_DV)
    o = rms_norm(o, subln_g) * (1.0 - lambda_init)
    return o.reshape(B, S, DA_HEADS * DA_DV)


def compress(xs, pos_emb, w1, w2):
    S = xs.shape[1]
    n_cmp = (S - CMP_LEN) // CMP_STRIDE + 1
    idx = jnp.arange(n_cmp)[:, None] * CMP_STRIDE + jnp.arange(CMP_LEN)[None, :]
    blk = xs[:, idx] + pos_emb[None, None, :, None, :]
    hid = jax.nn.gelu(jnp.einsum('bnlgd,ldh->bngh', blk, w1.reshape(CMP_LEN, NSA_DK, CMP_HIDDEN)), approximate=False)
    return jnp.einsum('bngh,hd->bngd', hid, w2)


def nsa_attention(q, kv, gate_logits, cos, sin, cmp_pos, ck1, ck2, cv1, cv2):
    B, S = q.shape[:2]
    G = NSA_GROUPS
    q = partial_rope(q, cos, sin).reshape(B, S, G, NSA_HPG, NSA_DK)
    k_cmp = partial_rope(kv[:, :, 0], cos, sin)
    v_cmp = kv[:, :, 1]
    k_sel = partial_rope(kv[:, :, 2], cos, sin)
    v_sel = kv[:, :, 3]
    k_win = partial_rope(kv[:, :, 4], cos, sin)
    v_win = kv[:, :, 5]
    gates = jax.nn.sigmoid(gate_logits)

    n_cmp = (S - CMP_LEN) // CMP_STRIDE + 1
    n_sel = S // SEL_LEN
    topk = min(SEL_TOPK, n_sel)
    per_sel = SEL_LEN // CMP_STRIDE
    pad = n_sel * per_sel - n_cmp
    kc = compress(k_cmp, cmp_pos, ck1, ck2)
    vc = compress(v_cmp, cmp_pos, cv1, cv2)
    cmp_end = jnp.arange(n_cmp) * CMP_STRIDE + CMP_LEN - 1
    blk_ids = jnp.arange(n_sel)
    sel_start = blk_ids * SEL_LEN

    def to_blocks(a):
        return a.reshape(B, n_sel, SEL_LEN, G, NSA_DK).transpose(0, 3, 1, 2, 4)

    ks_blk = to_blocks(k_sel)
    vs_blk = to_blocks(v_sel)
    kw_pad = jnp.pad(k_win, ((0, 0), (WINDOW, 0), (0, 0), (0, 0)))
    vw_pad = jnp.pad(v_win, ((0, 0), (WINDOW, 0), (0, 0), (0, 0)))
    b_idx = jnp.arange(B)[:, None, None, None]
    g_idx = jnp.arange(G)[None, None, :, None]
    scale = NSA_DK ** -0.5

    def block(c):
        q0 = c * Q_BLOCK
        t = q0 + jnp.arange(Q_BLOCK)
        qb = lax.dynamic_slice_in_dim(q, q0, Q_BLOCK, axis=1)
        gb = lax.dynamic_slice_in_dim(gates, q0, Q_BLOCK, axis=1)
        s_c = jnp.einsum('bqghd,bngd->bqghn', qb, kc) * scale
        m_c = (cmp_end[None, :] <= t[:, None])[None, :, None, None, :]
        p_c = masked_softmax(s_c, m_c)
        o_c = jnp.einsum('bqghn,bngd->bqghd', p_c.astype(qb.dtype), vc)
        imp = jnp.pad(p_c.sum(axis=3), ((0, 0), (0, 0), (0, 0), (0, pad)))
        imp = imp.reshape(B, Q_BLOCK, G, n_sel, per_sel).sum(-1)
        elig = sel_start[None, :] <= t[:, None]
        cur = (t // SEL_LEN)[:, None]
        forced = (blk_ids[None, :] == 0) | (blk_ids[None, :] == cur) | (blk_ids[None, :] == cur - 1)
        bonus = jnp.where(forced, FORCE_BONUS, 0.0)
        score = jnp.where(elig[None, :, None, :], imp + bonus[None, :, None, :], -1.0)
        _, idx = lax.top_k(score, topk)
        valid = jnp.take_along_axis(jnp.broadcast_to(elig[None, :, None, :], score.shape), idx, axis=-1)
        k_g = ks_blk[b_idx, g_idx, idx].reshape(B, Q_BLOCK, G, topk * SEL_LEN, NSA_DK)
        v_g = vs_blk[b_idx, g_idx, idx].reshape(B, Q_BLOCK, G, topk * SEL_LEN, NSA_DK)
        tok = idx[..., None] * SEL_LEN + jnp.arange(SEL_LEN)
        m_s = ((tok <= t[None, :, None, None, None]) & valid[..., None]).reshape(B, Q_BLOCK, G, 1, topk * SEL_LEN)
        s_s = jnp.einsum('bqghd,bqgkd->bqghk', qb, k_g) * scale
        o_s = jnp.einsum('bqghk,bqgkd->bqghd', masked_softmax(s_s, m_s).astype(qb.dtype), v_g)
        kw = lax.dynamic_slice_in_dim(kw_pad, q0, Q_BLOCK + WINDOW, axis=1)
        vw = lax.dynamic_slice_in_dim(vw_pad, q0, Q_BLOCK + WINDOW, axis=1)
        kpos = q0 - WINDOW + jnp.arange(Q_BLOCK + WINDOW)
        dist = t[:, None] - kpos[None, :]
        m_w = ((dist >= 0) & (dist < WINDOW) & (kpos[None, :] >= 0))[None, :, None, None, :]
        s_w = jnp.einsum('bqghd,bkgd->bqghk', qb, kw) * scale
        o_w = jnp.einsum('bqghk,bkgd->bqghd', masked_softmax(s_w, m_w).astype(qb.dtype), vw)
        o = gb[..., 0:1] * o_c + gb[..., 1:2] * o_s + gb[..., 2:3] * o_w
        return o.reshape(B, Q_BLOCK, NSA_HEADS * NSA_DK)

    o = lax.map(block, jnp.arange(S // Q_BLOCK))
    return jnp.moveaxis(o, 0, 1).reshape(B, S, NSA_HEADS * NSA_DK)


def memory_cross_attention(hn, memn, wq, wkv, wo):
    B, S, _ = hn.shape
    M = memn.shape[1]
    q = (hn @ wq).reshape(B, S, X_HEADS, X_DH)
    kv = (memn @ wkv).reshape(B, M, 2, X_HEADS, X_DH)
    s = jnp.einsum('bshd,bmhd->bhsm', q, kv[:, :, 0]) * (X_DH ** -0.5)
    p = jax.nn.softmax(s.astype(jnp.float32), axis=-1).astype(hn.dtype)
    o = jnp.einsum('bhsm,bmhd->bshd', p, kv[:, :, 1])
    return o.reshape(B, S, X_HEADS * X_DH) @ wo


def peer_ffn(xn, wq, keys1, keys2, U, V):
    B, S, D = xn.shape
    C = PEER_CHUNK
    K = PEER_TOPK
    half = PEER_DQ // 2

    def chunk(c):
        xc = lax.dynamic_slice_in_dim(xn, c * C, C, axis=1)
        q = (xc @ wq).reshape(B, C, PEER_HEADS, PEER_DQ)
        s1 = jnp.einsum('bchd,nd->bchn', q[..., :half], keys1).astype(jnp.float32)
        s2 = jnp.einsum('bchd,nd->bchn', q[..., half:], keys2).astype(jnp.float32)
        v1, i1 = lax.top_k(s1, K)
        v2, i2 = lax.top_k(s2, K)
        cand = (v1[..., :, None] + v2[..., None, :]).reshape(B, C, PEER_HEADS, K * K)
        sc, ci = lax.top_k(cand, K)
        e = jnp.take_along_axis(i1, ci // K, axis=-1) * PEER_NKEYS + jnp.take_along_axis(i2, ci % K, axis=-1)
        g = jax.nn.softmax(sc, axis=-1).astype(xc.dtype)
        a = jax.nn.gelu(jnp.einsum('bchkd,bcd->bchk', U[e], xc), approximate=False)
        return jnp.einsum('bchk,bchkd->bcd', g * a, V[e])

    o = lax.map(chunk, jnp.arange(S // C))
    return jnp.moveaxis(o, 0, 1).reshape(B, S, D)


def setup_inputs(seed: int = 0) -> dict:
    key = jax.random.key(seed)
    ks = iter(jax.random.split(key, 40))

    def nrm(shape, scale):
        return jax.random.normal(next(ks), shape, jnp.float32) * scale

    def gain(shape):
        return 1.0 + nrm(shape, 0.02)

    L = DEPTH
    D = D_MODEL
    x = nrm((BATCH, SEQ, D), 1.0)
    mem = nrm((BATCH, MEM_LEN, D), 1.0)
    off = jax.random.randint(next(ks), (BATCH,), 0, 1024)
    positions = (jnp.arange(SEQ, dtype=jnp.int32)[None, :] + off[:, None]).astype(jnp.int32)
    return {
        'x': x,
        'mem': mem,
        'positions': positions,
        'norm_mix_g': gain((L, D)),
        'w_in': nrm((L, D, IN_COLS), D ** -0.5),
        'da_lambda_q1': nrm((L, DA_DK), 0.1),
        'da_lambda_k1': nrm((L, DA_DK), 0.1),
        'da_lambda_q2': nrm((L, DA_DK), 0.1),
        'da_lambda_k2': nrm((L, DA_DK), 0.1),
        'da_subln_g': gain((L, DA_DV)),
        'nsa_cmp_pos': nrm((L, CMP_LEN, NSA_DK), 0.02),
        'nsa_cmpk_w1': nrm((L, CMP_LEN * NSA_DK, CMP_HIDDEN), (CMP_LEN * NSA_DK) ** -0.5),
        'nsa_cmpk_w2': nrm((L, CMP_HIDDEN, NSA_DK), CMP_HIDDEN ** -0.5),
        'nsa_cmpv_w1': nrm((L, CMP_LEN * NSA_DK, CMP_HIDDEN), (CMP_LEN * NSA_DK) ** -0.5),
        'nsa_cmpv_w2': nrm((L, CMP_HIDDEN, NSA_DK), CMP_HIDDEN ** -0.5),
        'w_branch_da': nrm((L, DA_HEADS * DA_DV, D), (DA_HEADS * DA_DV) ** -0.5),
        'w_branch_nsa': nrm((L, NSA_HEADS * NSA_DK, D), (NSA_HEADS * NSA_DK) ** -0.5),
        'w_out': nrm((L, D, D), D ** -0.5),
        'norm_x_g': gain((L, D)),
        'norm_mem_g': gain((L, D)),
        'w_xq': nrm((L, D, X_HEADS * X_DH), D ** -0.5),
        'w_xkv': nrm((L, D, 2 * X_HEADS * X_DH), D ** -0.5),
        'w_xo': nrm((L, X_HEADS * X_DH, D), (X_HEADS * X_DH) ** -0.5),
        'norm_ffn_g': gain((L, D)),
        'peer_wq': nrm((L, D, PEER_HEADS * PEER_DQ), D ** -0.5),
        'peer_keys1': nrm((L, PEER_NKEYS, PEER_DQ // 2), (PEER_DQ // 2) ** -0.5),
        'peer_keys2': nrm((L, PEER_NKEYS, PEER_DQ // 2), (PEER_DQ // 2) ** -0.5),
        'peer_u': nrm((L, PEER_EXPERTS, D), D ** -0.5),
        'peer_v': nrm((L, PEER_EXPERTS, D), 0.5),
        'norm_final_g': gain((D,)),
    }


def reference(x, mem, positions, norm_mix_g, w_in, da_lambda_q1, da_lambda_k1, da_lambda_q2, da_lambda_k2,
              da_subln_g, nsa_cmp_pos, nsa_cmpk_w1, nsa_cmpk_w2, nsa_cmpv_w1, nsa_cmpv_w2, w_branch_da,
              w_branch_nsa, w_out, norm_x_g, norm_mem_g, w_xq, w_xkv, w_xo, norm_ffn_g, peer_wq, peer_keys1,
              peer_keys2, peer_u, peer_v, norm_final_g):
    B, S, _ = x.shape
    cos, sin = rope_tables(positions, x.dtype)
    h = x
    for l in range(DEPTH):
        xn = rms_norm(h, norm_mix_g[l])
        z = xn @ w_in[l]
        zq, zk, zv, nq, nkv, ng, bg = jnp.split(z, SPLIT_POINTS, axis=-1)
        dq = partial_rope(zq.reshape(B, S, DA_HEADS, 2, DA_DK), cos, sin)
        dk = partial_rope(zk.reshape(B, S, DA_HEADS, 2, DA_DK), cos, sin)
        dv = zv.reshape(B, S, DA_HEADS, DA_DV)
        lambda_init = 0.8 - 0.6 * math.exp(-0.3 * l)
        lam = (jnp.exp(jnp.sum((da_lambda_q1[l] * da_lambda_k1[l]).astype(jnp.float32)))
               - jnp.exp(jnp.sum((da_lambda_q2[l] * da_lambda_k2[l]).astype(jnp.float32))) + lambda_init)
        o_da = diff_attention(dq, dk, dv, lam, lambda_init, da_subln_g[l])
        o_nsa = nsa_attention(nq.reshape(B, S, NSA_HEADS, NSA_DK),
                              nkv.reshape(B, S, 6, NSA_GROUPS, NSA_DK),
                              ng.reshape(B, S, NSA_GROUPS, NSA_HPG, 3), cos, sin,
                              nsa_cmp_pos[l], nsa_cmpk_w1[l], nsa_cmpk_w2[l], nsa_cmpv_w1[l], nsa_cmpv_w2[l])
        gates = jax.nn.sigmoid(bg.reshape(B, S, 2, D_MODEL))
        mix = gates[:, :, 0] * (o_da @ w_branch_da[l]) + gates[:, :, 1] * (o_nsa @ w_branch_nsa[l])
        h = h + mix @ w_out[l]
        h = h + memory_cross_attention(rms_norm(h, norm_x_g[l]), rms_norm(mem, norm_mem_g[l]),
                                       w_xq[l], w_xkv[l], w_xo[l])
        h = h + peer_ffn(rms_norm(h, norm_ffn_g[l]), peer_wq[l], peer_keys1[l], peer_keys2[l],
                         peer_u[l], peer_v[l])
    return rms_norm(h, norm_final_g)
```

```python
import functools
import math

import jax
import jax.numpy as jnp
import numpy as np
from jax import lax
from jax.experimental import pallas as pl
from jax.experimental.pallas import tpu as pltpu

F32 = jnp.float32
BF16 = jnp.bfloat16

D_MODEL = 2048
EPS = 1e-6
NEG = -1e30
ROPE_THETA = 500000.0
DA_HEADS = 8
DA_DK = 64
DA_DV = 128
NSA_GROUPS = 4
NSA_HPG = 4
NSA_DK = 64
CMP_LEN = 32
CMP_STRIDE = 16
CMP_HIDDEN = 128
SEL_LEN = 64
SEL_TOPK = 16
WINDOW = 512
FORCE_BONUS = 100.0
ROPE_DIM = 16
X_HEADS = 4
X_DH = 128
PEER_HEADS = 8
PEER_NKEYS = 128
PEER_DQ = 256
PEER_TOPK = 16

LANES = 128
VMEM_LIMIT = 56 * 1024 * 1024

C_ZQ = 0
C_ZK = 1024
C_NQ = 2048
C_KCMP = 3072
C_KSEL = 3328
C_KWIN = 3584
C_NGB = 3840
NG_LANE0 = 16
N_ROPE_COLS = 4096
C_ZV = 4096
C_VCMP = 5120
C_VSEL = 5376
C_VWIN = 5632
C_BG = 6144
N_COLS = 10240
TN_IN = 512
NSA_COLS = 1024


def _params(sem):
    return pltpu.CompilerParams(dimension_semantics=sem, vmem_limit_bytes=VMEM_LIMIT)


def _const_spec(shape):
    n = len(shape)
    return pl.BlockSpec(shape, lambda *_: (0,) * n, pipeline_mode=pl.Buffered(1))


def _gelu(x):
    return 0.5 * x * (1.0 + lax.erf(x * (2.0 ** -0.5)))


def _rms(x, g):
    ms = jnp.mean(x * x, axis=-1, keepdims=True)
    return x * lax.rsqrt(ms + EPS) * g


def _inproj_kernel(x_ref, g_ref, w_ref, c_ref, sp_ref, sm_ref, o_ref, xn_ref):
    j = pl.program_id(1)

    @pl.when(j == 0)
    def _():
        xn_ref[...] = _rms(x_ref[...], g_ref[...]).astype(BF16)

    z = jnp.dot(xn_ref[...], w_ref[...], preferred_element_type=F32)

    @pl.when(j < N_ROPE_COLS // TN_IN)
    def _():
        c = c_ref[...]
        sp = sp_ref[...]
        sm = sm_ref[...]
        for k in range(TN_IN // LANES):
            zc = z[:, k * LANES:(k + 1) * LANES]
            r = zc * c + pltpu.roll(zc, 8, 1) * sp + pltpu.roll(zc, LANES - 8, 1) * sm
            o_ref[:, k * LANES:(k + 1) * LANES] = r.astype(o_ref.dtype)

    @pl.when(j >= N_ROPE_COLS // TN_IN)
    def _():
        o_ref[...] = z.astype(o_ref.dtype)


def _inproj(x, g, w, rope_c, rope_sp, rope_sm, tm=512):
    S = x.shape[0]
    return pl.pallas_call(
        _inproj_kernel,
        grid=(S // tm, N_COLS // TN_IN),
        in_specs=[
            pl.BlockSpec((tm, D_MODEL), lambda i, j: (i, 0)),
            pl.BlockSpec((1, D_MODEL), lambda i, j: (0, 0)),
            pl.BlockSpec((D_MODEL, TN_IN), lambda i, j: (0, j)),
            pl.BlockSpec((tm, LANES), lambda i, j: (i, 0)),
            pl.BlockSpec((tm, LANES), lambda i, j: (i, 0)),
            pl.BlockSpec((tm, LANES), lambda i, j: (i, 0)),
        ],
        out_specs=pl.BlockSpec((tm, TN_IN), lambda i, j: (i, j)),
        out_shape=jax.ShapeDtypeStruct((S, N_COLS), BF16),
        scratch_shapes=[pltpu.VMEM((tm, D_MODEL), BF16)],
        compiler_params=_params(("parallel", "arbitrary")),
        name="inproj",
    )(x, g, w, rope_c, rope_sp, rope_sm)


def _pack_w_in(w_in):
    zq, zk, zv, nq, nkv, ng, bg = jnp.split(
        w_in, np.cumsum([1024, 1024, 1024, 1024, 1536, 48]).tolist(), axis=1)
    D = w_in.shape[0]
    nq = nq.reshape(D, 2, 2, NSA_HPG, NSA_DK).transpose(0, 1, 3, 2, 4).reshape(D, 1024)
    nkv = nkv.reshape(D, 6, 256)
    ngb = jnp.concatenate([jnp.zeros((D, NG_LANE0), w_in.dtype), ng,
                           jnp.zeros((D, 256 - NG_LANE0 - 48), w_in.dtype)], axis=1)
    pad = jnp.zeros((D, C_BG - C_VWIN - 256), w_in.dtype)
    w = jnp.concatenate([zq, zk, nq, nkv[:, 0], nkv[:, 2], nkv[:, 4], ngb,
                         zv, nkv[:, 1], nkv[:, 3], nkv[:, 5], pad, bg], axis=1)
    return w.astype(BF16)


def _rope_tables(positions):
    inv = ROPE_THETA ** (-jnp.arange(0, ROPE_DIM, 2, dtype=F32) / ROPE_DIM)
    ang = positions.astype(F32)[:, None] * inv
    cos, sin = jnp.cos(ang), jnp.sin(ang)
    S = positions.shape[0]
    one = jnp.ones((S, 48), F32)
    zero = jnp.zeros((S, 48), F32)
    z8 = jnp.zeros((S, 8), F32)
    c64 = jnp.concatenate([cos, cos, one], axis=1)
    sp64 = jnp.concatenate([z8, sin, zero], axis=1)
    sm64 = jnp.concatenate([-sin, z8, zero], axis=1)
    return (jnp.tile(c64, (1, 2)), jnp.tile(sp64, (1, 2)), jnp.tile(sm64, (1, 2)))


def _online_softmax_step(s, v, m_ref, l_ref, acc_ref):
    m_prev = m_ref[...]
    m_new = jnp.maximum(m_prev, jnp.max(s, axis=-1, keepdims=True))
    alpha = jnp.exp(m_prev - m_new)
    p = jnp.exp(s - m_new[:, 0:1])
    l_ref[...] = alpha * l_ref[...] + jnp.sum(p, axis=-1, keepdims=True)
    acc_ref[...] = alpha * acc_ref[...] + jnp.dot(p.astype(BF16), v, preferred_element_type=F32)
    m_ref[...] = m_new


def _da_kernel(lam_ref, q_ref, k_ref, v_ref, g_ref, o_ref, qs_ref, m_ref, l_ref, acc_ref,
               *, tq, tk, out_scale):
    qi = pl.program_id(1)
    ki = pl.program_id(2)
    last = (qi * tq + tq - 1) // tk

    @pl.when(ki == 0)
    def _():
        q = q_ref[...].astype(F32) * (DA_DK ** -0.5)
        lane = lax.broadcasted_iota(jnp.int32, q.shape, 1)
        qs_ref[0:tq, :] = jnp.where(lane < DA_DK, q, 0.0).astype(BF16)
        qs_ref[tq:2 * tq, :] = jnp.where(lane >= DA_DK, q, 0.0).astype(BF16)
        m_ref[...] = jnp.full(m_ref.shape, NEG, F32)
        l_ref[...] = jnp.zeros(l_ref.shape, F32)
        acc_ref[...] = jnp.zeros(acc_ref.shape, F32)

    def step(masked):
        s = lax.dot_general(qs_ref[...], k_ref[...], (((1,), (1,)), ((), ())),
                            preferred_element_type=F32)
        if masked:
            s = s.reshape(2, tq, tk)
            row = lax.broadcasted_iota(jnp.int32, (tq, tk), 0)
            col = lax.broadcasted_iota(jnp.int32, (tq, tk), 1)
            s = jnp.where((ki * tk + col <= qi * tq + row)[None], s, NEG).reshape(2 * tq, tk)
        _online_softmax_step(s, v_ref[...], m_ref, l_ref, acc_ref)

    crosses = (ki + 1) * tk - 1 > qi * tq

    @pl.when(jnp.logical_and(ki <= last, crosses))
    def _():
        step(True)

    @pl.when(jnp.logical_and(ki <= last, jnp.logical_not(crosses)))
    def _():
        step(False)

    @pl.when(ki == last)
    def _():
        o = acc_ref[...] / l_ref[...]
        d = o[0:tq] - lam_ref[...] * o[tq:2 * tq]
        o_ref[...] = (_rms(d, g_ref[...]) * out_scale).astype(o_ref.dtype)


def _diff_attention(z, lam_row, subln_g, lambda_init, tq=256, tk=512):
    S = z.shape[0]
    tk = min(tk, S)
    nq, nk = S // tq, S // tk

    def kv_map(off):
        return lambda h, qi, ki: (jnp.minimum(ki, (qi * tq + tq - 1) // tk), off + h)

    return pl.pallas_call(
        functools.partial(_da_kernel, tq=tq, tk=tk, out_scale=1.0 - lambda_init),
        grid=(DA_HEADS, nq, nk),
        in_specs=[
            pl.BlockSpec((1, LANES), lambda h, qi, ki: (0, 0)),
            pl.BlockSpec((tq, LANES), lambda h, qi, ki: (qi, C_ZQ // LANES + h)),
            pl.BlockSpec((tk, LANES), kv_map(C_ZK // LANES)),
            pl.BlockSpec((tk, LANES), kv_map(C_ZV // LANES)),
            pl.BlockSpec((1, LANES), lambda h, qi, ki: (0, 0)),
        ],
        out_specs=pl.BlockSpec((tq, LANES), lambda h, qi, ki: (qi, h)),
        out_shape=jax.ShapeDtypeStruct((S, DA_HEADS * DA_DV), BF16),
        scratch_shapes=[
            pltpu.VMEM((2 * tq, LANES), BF16),
            pltpu.VMEM((2 * tq, LANES), F32),
            pltpu.VMEM((2 * tq, LANES), F32),
            pltpu.VMEM((2 * tq, LANES), F32),
        ],
        compiler_params=_params(("parallel", "parallel", "arbitrary")),
        name="diff_attn",
    )(lam_row, z, z, z, subln_g)


def _compress_kernel(x_ref, pos_ref, w1_ref, w2_ref, o_ref):
    n = x_ref.shape[3]
    w1a = w1_ref[0, 0:CMP_STRIDE * NSA_DK, :]
    w1b = w1_ref[0, CMP_STRIDE * NSA_DK:, :]
    c = jnp.dot(pos_ref[...], w1_ref[0], preferred_element_type=F32)[0:1]
    outs = []
    for gg in range(2):
        xs = x_ref[0, 0, gg]
        y1 = jnp.dot(xs, w1a, preferred_element_type=F32)
        y2 = jnp.dot(xs, w1b, preferred_element_type=F32)
        hid = _gelu(y1 + pltpu.roll(y2, n - 1, 0) + c)
        outs.append(jnp.dot(hid.astype(BF16), w2_ref[0], preferred_element_type=F32))
    o_ref[0] = jnp.concatenate(outs, axis=1).astype(o_ref.dtype)


def _compress(z, pos, w1, w2):
    S = z.shape[0]
    n = S // CMP_STRIDE

    def blocks(c0):
        a = z[:, c0:c0 + 256].reshape(n, CMP_STRIDE, 2, 2, NSA_DK)
        return a.transpose(2, 3, 0, 1, 4).reshape(2, 2, n, CMP_STRIDE * NSA_DK)

    xs = jnp.stack([blocks(C_KCMP), blocks(C_VCMP)])
    pos8 = jnp.broadcast_to(pos.reshape(1, CMP_LEN * NSA_DK), (8, CMP_LEN * NSA_DK)).astype(BF16)
    return pl.pallas_call(
        _compress_kernel,
        grid=(2, 2),
        in_specs=[
            pl.BlockSpec((1, 1, 2, n, CMP_STRIDE * NSA_DK), lambda c, p: (c, p, 0, 0, 0)),
            pl.BlockSpec((8, CMP_LEN * NSA_DK), lambda c, p: (0, 0)),
            pl.BlockSpec((1, CMP_LEN * NSA_DK, CMP_HIDDEN), lambda c, p: (c, 0, 0)),
            pl.BlockSpec((1, CMP_HIDDEN, NSA_DK), lambda c, p: (c, 0, 0)),
        ],
        out_specs=pl.BlockSpec((1, n, LANES), lambda c, p: (c, 0, p)),
        out_shape=jax.ShapeDtypeStruct((2, n, 256), BF16),
        compiler_params=_params(("parallel", "parallel")),
        name="nsa_compress",
    )(xs, pos8, w1.astype(BF16), w2.astype(BF16))


def _stack_group_queries(q_ref, qs_ref, tq):
    lane = lax.broadcasted_iota(jnp.int32, (tq, LANES), 1)
    for h in range(NSA_HPG):
        q = q_ref[:, h * LANES:(h + 1) * LANES].astype(F32) * (NSA_DK ** -0.5)
        qs_ref[0, h * tq:(h + 1) * tq, :] = jnp.where(lane < NSA_DK, q, 0.0).astype(BF16)
        qs_ref[1, h * tq:(h + 1) * tq, :] = jnp.where(lane >= NSA_DK, q, 0.0).astype(BF16)


def _merge_pair_heads(o0, o1, o_ref, tq):
    lane = lax.broadcasted_iota(jnp.int32, (tq, LANES), 1)
    for h in range(NSA_HPG):
        o_ref[:, h * LANES:(h + 1) * LANES] = jnp.where(
            lane < NSA_DK, o0[h * tq:(h + 1) * tq], o1[h * tq:(h + 1) * tq]).astype(o_ref.dtype)


def _cmp_kernel(q_ref, kc_ref, vc_ref, pool_ref, o_ref, sel_ref, qs_ref, *, tq):
    qi = pl.program_id(1)
    n = kc_ref.shape[1]
    _stack_group_queries(q_ref, qs_ref, tq)
    t = qi * tq + lax.broadcasted_iota(jnp.int32, (tq, n), 0)
    cmp_end = lax.broadcasted_iota(jnp.int32, (tq, n), 1) * CMP_STRIDE + (CMP_LEN - 1)
    vis = (cmp_end <= t)[None]
    tb = qi * tq + lax.broadcasted_iota(jnp.int32, (tq, LANES), 0)
    blk = lax.broadcasted_iota(jnp.int32, (tq, LANES), 1)
    elig = blk * SEL_LEN <= tb
    cur = tb // SEL_LEN
    forced = (blk == 0) | (blk == cur) | (blk == cur - 1)
    outs = []
    for gg in range(2):
        s = lax.dot_general(qs_ref[gg], kc_ref[0], (((1,), (1,)), ((), ())),
                            preferred_element_type=F32).reshape(NSA_HPG, tq, n)
        s = jnp.where(vis, s, NEG)
        m = jnp.max(s, axis=-1, keepdims=True)
        e = jnp.where(vis, jnp.exp(s - m), 0.0)
        den = jnp.sum(e, axis=-1, keepdims=True)
        p = e / jnp.where(den == 0.0, 1.0, den)
        outs.append(jnp.dot(p.reshape(NSA_HPG * tq, n).astype(BF16), vc_ref[0],
                            preferred_element_type=F32))
        imp = jnp.sum(p, axis=0)
        hi = imp.astype(BF16)
        r1 = imp - hi.astype(F32)
        mid = r1.astype(BF16)
        lo = (r1 - mid.astype(F32)).astype(BF16)
        pooled = (jnp.dot(hi, pool_ref[...], preferred_element_type=F32)
                  + jnp.dot(mid, pool_ref[...], preferred_element_type=F32)
                  + jnp.dot(lo, pool_ref[...], preferred_element_type=F32))
        score = jnp.where(elig, pooled + jnp.where(forced, FORCE_BONUS, 0.0), -1.0)
        sel = jnp.zeros((tq, LANES), F32)
        for _ in range(SEL_TOPK):
            mx = jnp.max(score, axis=-1, keepdims=True)
            first = jnp.min(jnp.where(score == mx, blk, LANES), axis=-1, keepdims=True)
            hit = blk == first
            sel = jnp.where(hit & elig, 1.0, sel)
            score = jnp.where(hit, -3e38, score)
        sel_ref[:, gg * LANES:(gg + 1) * LANES] = sel.astype(sel_ref.dtype)
    _merge_pair_heads(outs[0], outs[1], o_ref, tq)


def _nsa_compressed(z, kvc, tq=128):
    S = z.shape[0]
    n = S // CMP_STRIDE
    pool = (np.arange(n)[:, None] // (SEL_LEN // CMP_STRIDE) == np.arange(LANES)[None, :])
    pool = jnp.asarray(pool, BF16)
    return pl.pallas_call(
        functools.partial(_cmp_kernel, tq=tq),
        grid=(2, S // tq),
        in_specs=[
            pl.BlockSpec((tq, 512), lambda p, qi: (qi, C_NQ // 512 + p)),
            pl.BlockSpec((1, n, LANES), lambda p, qi: (0, 0, p)),
            pl.BlockSpec((1, n, LANES), lambda p, qi: (1, 0, p)),
            pl.BlockSpec((n, LANES), lambda p, qi: (0, 0)),
        ],
        out_specs=[
            pl.BlockSpec((tq, 512), lambda p, qi: (qi, p)),
            pl.BlockSpec((tq, 256), lambda p, qi: (qi, p)),
        ],
        out_shape=[
            jax.ShapeDtypeStruct((S, NSA_COLS), BF16),
            jax.ShapeDtypeStruct((S, 512), BF16),
        ],
        scratch_shapes=[pltpu.VMEM((2, NSA_HPG * tq, LANES), BF16)],
        compiler_params=_params(("parallel", "parallel")),
        name="nsa_cmp_topk",
    )(z, kvc, kvc, pool)


def _sel_kernel(q_ref, k_ref, v_ref, sel_ref, ex_ref, o_ref, qs_ref, m_ref, l_ref, acc_ref,
                *, tq, tk):
    qi = pl.program_id(1)
    ki = pl.program_id(2)
    last = (qi * tq + tq - 1) // tk

    @pl.when(ki == 0)
    def _():
        _stack_group_queries(q_ref, qs_ref, tq)
        m_ref[...] = jnp.full(m_ref.shape, NEG, F32)
        l_ref[...] = jnp.zeros(l_ref.shape, F32)
        acc_ref[...] = jnp.zeros(acc_ref.shape, F32)

    @pl.when(ki <= last)
    def _():
        row = lax.broadcasted_iota(jnp.int32, (tq, tk), 0)
        col = lax.broadcasted_iota(jnp.int32, (tq, tk), 1)
        causal = ki * tk + col <= qi * tq + row
        for gg in range(2):
            hit = jnp.dot(sel_ref[:, gg * LANES:(gg + 1) * LANES], ex_ref[...],
                          preferred_element_type=F32)
            bias = jnp.where(causal & (hit > 0.5), 0.0, NEG)
            s = lax.dot_general(qs_ref[gg], k_ref[...], (((1,), (1,)), ((), ())),
                                preferred_element_type=F32)
            s = (s.reshape(NSA_HPG, tq, tk) + bias[None]).reshape(NSA_HPG * tq, tk)
            _online_softmax_step(s, v_ref[...], m_ref.at[gg], l_ref.at[gg], acc_ref.at[gg])

    @pl.when(ki == last)
    def _():
        _merge_pair_heads(acc_ref[0] / l_ref[0], acc_ref[1] / l_ref[1], o_ref, tq)


def _nsa_selected(z, sel, tq=128, tk=512):
    S = z.shape[0]
    tk = min(tk, S)
    expand = jnp.asarray(np.arange(LANES)[:, None] == np.arange(S)[None, :] // SEL_LEN, BF16)

    def kv_map(off):
        return lambda p, qi, ki: (jnp.minimum(ki, (qi * tq + tq - 1) // tk), off + p)

    return pl.pallas_call(
        functools.partial(_sel_kernel, tq=tq, tk=tk),
        grid=(2, S // tq, S // tk),
        in_specs=[
            pl.BlockSpec((tq, 512), lambda p, qi, ki: (qi, C_NQ // 512 + p)),
            pl.BlockSpec((tk, LANES), kv_map(C_KSEL // LANES)),
            pl.BlockSpec((tk, LANES), kv_map(C_VSEL // LANES)),
            pl.BlockSpec((tq, 256), lambda p, qi, ki: (qi, p)),
            pl.BlockSpec((LANES, tk),
                         lambda p, qi, ki: (0, jnp.minimum(ki, (qi * tq + tq - 1) // tk))),
        ],
        out_specs=pl.BlockSpec((tq, 512), lambda p, qi, ki: (qi, p)),
        out_shape=jax.ShapeDtypeStruct((S, NSA_COLS), BF16),
        scratch_shapes=[
            pltpu.VMEM((2, NSA_HPG * tq, LANES), BF16),
            pltpu.VMEM((2, NSA_HPG * tq, LANES), F32),
            pltpu.VMEM((2, NSA_HPG * tq, LANES), F32),
            pltpu.VMEM((2, NSA_HPG * tq, LANES), F32),
        ],
        compiler_params=_params(("parallel", "parallel", "arbitrary")),
        name="nsa_selected",
    )(z, z, z, sel, expand)


def _win_kernel(q_ref, *refs, tq, nprev):
    k_refs = refs[:nprev + 1]
    v_refs = refs[nprev + 1:2 * nprev + 2]
    o_ref, qs_ref = refs[2 * nprev + 2:]
    qi = pl.program_id(1)
    nk = (nprev + 1) * tq
    _stack_group_queries(q_ref, qs_ref, tq)
    k = jnp.concatenate([r[...] for r in k_refs], axis=0)
    v = jnp.concatenate([r[...] for r in v_refs], axis=0)
    t = qi * tq + lax.broadcasted_iota(jnp.int32, (tq, nk), 0)
    kpos = (qi - nprev) * tq + lax.broadcasted_iota(jnp.int32, (tq, nk), 1)
    dist = t - kpos
    bias = jnp.where((dist >= 0) & (dist < WINDOW) & (kpos >= 0), 0.0, NEG)[None]
    outs = []
    for gg in range(2):
        s = lax.dot_general(qs_ref[gg], k, (((1,), (1,)), ((), ())), preferred_element_type=F32)
        s = s.reshape(NSA_HPG, tq, nk) + bias
        m = jnp.max(s, axis=-1, keepdims=True)
        e = jnp.exp(s - m)
        p = e / jnp.sum(e, axis=-1, keepdims=True)
        outs.append(jnp.dot(p.reshape(NSA_HPG * tq, nk).astype(BF16), v,
                            preferred_element_type=F32))
    _merge_pair_heads(outs[0], outs[1], o_ref, tq)


def _nsa_window(z, tq=256):
    S = z.shape[0]
    nprev = WINDOW // tq

    def kv_spec(off, back):
        return pl.BlockSpec((tq, LANES), lambda p, qi: (jnp.maximum(qi - back, 0), off + p))

    backs = list(range(nprev, -1, -1))
    return pl.pallas_call(
        functools.partial(_win_kernel, tq=tq, nprev=nprev),
        grid=(2, S // tq),
        in_specs=[pl.BlockSpec((tq, 512), lambda p, qi: (qi, C_NQ // 512 + p))]
        + [kv_spec(C_KWIN // LANES, b) for b in backs]
        + [kv_spec(C_VWIN // LANES, b) for b in backs],
        out_specs=pl.BlockSpec((tq, 512), lambda p, qi: (qi, p)),
        out_shape=jax.ShapeDtypeStruct((S, NSA_COLS), BF16),
        scratch_shapes=[pltpu.VMEM((2, NSA_HPG * tq, LANES), BF16)],
        compiler_params=_params(("parallel", "parallel")),
        name="nsa_window",
    )(z, *([z] * (2 * nprev + 2)))


def _memkv_kernel(mem_ref, g_ref, w_ref, o_ref):
    memn = _rms(mem_ref[...], g_ref[...]).astype(BF16)
    o_ref[...] = jnp.dot(memn, w_ref[...], preferred_element_type=F32).astype(o_ref.dtype)


def _memkv(mem, g, w):
    M = mem.shape[0]
    return pl.pallas_call(
        _memkv_kernel,
        out_shape=jax.ShapeDtypeStruct((M, 2 * X_HEADS * X_DH), BF16),
        compiler_params=pltpu.CompilerParams(vmem_limit_bytes=VMEM_LIMIT),
        name="mem_kv",
    )(mem, g, w)


def _mid_kernel(x_ref, oda_ref, oc_ref, os_ref, ow_ref, ng_ref, bg0_ref, bg1_ref, eg_ref,
                wda_ref, wnsa_ref, wout_ref, gx_ref, wxq_ref, mkv_ref, wxo_ref, gf_ref,
                h_ref, xnt_ref):
    ng = ng_ref[...]
    o_nsa = jnp.zeros(oc_ref.shape, F32)
    for c, br in enumerate((oc_ref, os_ref, ow_ref)):
        gate = jax.nn.sigmoid(jnp.dot(ng, eg_ref[c], preferred_element_type=F32))
        o_nsa = o_nsa + gate * br[...].astype(F32)
    y_da = jnp.dot(oda_ref[...], wda_ref[...], preferred_element_type=F32)
    y_nsa = jnp.dot(o_nsa.astype(BF16), wnsa_ref[...], preferred_element_type=F32)
    mix = (jax.nn.sigmoid(bg0_ref[...].astype(F32)) * y_da
           + jax.nn.sigmoid(bg1_ref[...].astype(F32)) * y_nsa)
    h = x_ref[...] + jnp.dot(mix.astype(BF16), wout_ref[...], preferred_element_type=F32)
    hn = _rms(h, gx_ref[...]).astype(BF16)
    q = (jnp.dot(hn, wxq_ref[...], preferred_element_type=F32) * (X_DH ** -0.5)).astype(BF16)
    outs = []
    for hd in range(X_HEADS):
        k = mkv_ref[:, hd * X_DH:(hd + 1) * X_DH]
        v = mkv_ref[:, (X_HEADS + hd) * X_DH:(X_HEADS + hd + 1) * X_DH]
        s = lax.dot_general(q[:, hd * X_DH:(hd + 1) * X_DH], k, (((1,), (1,)), ((), ())),
                            preferred_element_type=F32)
        e = jnp.exp(s - jnp.max(s, axis=-1, keepdims=True))
        p = (e / jnp.sum(e, axis=-1, keepdims=True)).astype(BF16)
        outs.append(jnp.dot(p, v, preferred_element_type=F32))
    o = jnp.concatenate(outs, axis=1).astype(BF16)
    h = h + jnp.dot(o, wxo_ref[...], preferred_element_type=F32)
    h_ref[...] = h
    xnt_ref[...] = _rms(h, gf_ref[...]).T.astype(BF16)


def _mid(x, z, o_da, o_c, o_s, o_w, eg, wda, wnsa, wout, gx, wxq, mkv, wxo, gf, tm=256):
    S = x.shape[0]
    M = mkv.shape[0]
    row = lambda w: pl.BlockSpec((tm, w), lambda i: (i, 0))
    return pl.pallas_call(
        _mid_kernel,
        grid=(S // tm,),
        in_specs=[
            row(D_MODEL), row(1024), row(NSA_COLS), row(NSA_COLS), row(NSA_COLS),
            pl.BlockSpec((tm, LANES), lambda i: (i, C_NGB // LANES)),
            pl.BlockSpec((tm, D_MODEL), lambda i: (i, C_BG // D_MODEL)),
            pl.BlockSpec((tm, D_MODEL), lambda i: (i, C_BG // D_MODEL + 1)),
            _const_spec((3, LANES, NSA_COLS)),
            _const_spec((1024, D_MODEL)), _const_spec((NSA_COLS, D_MODEL)),
            _const_spec((D_MODEL, D_MODEL)),
            _const_spec((1, D_MODEL)), _const_spec((D_MODEL, X_HEADS * X_DH)),
            _const_spec((M, 2 * X_HEADS * X_DH)), _const_spec((X_HEADS * X_DH, D_MODEL)),
            _const_spec((1, D_MODEL)),
        ],
        out_specs=[
            pl.BlockSpec((tm, D_MODEL), lambda i: (i, 0)),
            pl.BlockSpec((D_MODEL, tm), lambda i: (0, i)),
        ],
        out_shape=[
            jax.ShapeDtypeStruct((S, D_MODEL), F32),
            jax.ShapeDtypeStruct((D_MODEL, S), BF16),
        ],
        compiler_params=_params(("parallel",)),
        name="merge_xattn",
    )(x, o_da, o_c, o_s, o_w, z, z, z, eg, wda, wnsa, wout, gx, wxq, mkv, wxo, gf)


def _gate_expand():
    e = np.zeros((3, LANES, NSA_COLS), np.float32)
    for c in range(3):
        for g in range(NSA_GROUPS):
            for h in range(NSA_HPG):
                col = ((g // 2) * NSA_HPG + h) * LANES + (g % 2) * NSA_DK
                e[c, NG_LANE0 + g * 12 + h * 3 + c, col:col + NSA_DK] = 1.0
    return jnp.asarray(e, BF16)


def _topk_rows(x, k):
    vals = []
    for _ in range(k):
        m = jnp.max(x, axis=0, keepdims=True)
        vals.append(m)
        x = jnp.where(x == m, -3e38, x)
    return vals


def _peer_pre_kernel(xnt_ref, wqt_ref, k1_ref, k2_ref, s1_ref, s2_ref, e1_ref, e2_ref, thr_ref):
    half = PEER_DQ // 2
    qt = jnp.dot(wqt_ref[...], xnt_ref[...], preferred_element_type=F32).astype(BF16)
    for h in range(PEER_HEADS):
        s1 = jnp.dot(k1_ref[...], qt[h * PEER_DQ:h * PEER_DQ + half], preferred_element_type=F32)
        s2 = jnp.dot(k2_ref[...], qt[h * PEER_DQ + half:(h + 1) * PEER_DQ],
                     preferred_element_type=F32)
        v1 = _topk_rows(s1, PEER_TOPK)
        v2 = jnp.concatenate(_topk_rows(s2, PEER_TOPK), axis=0)
        cand = jnp.concatenate([a + v2 for a in v1], axis=0)
        top = _topk_rows(cand, PEER_TOPK)
        thr = top[-1]
        zsum = jnp.sum(jnp.where(cand >= thr, jnp.exp(cand - top[0]), 0.0), axis=0, keepdims=True)
        s1_ref[h] = s1
        s2_ref[h] = s2
        e1_ref[h] = jnp.exp(s1 - v1[0])
        e2_ref[h] = jnp.exp(s2 - v2[0:1]) / zsum
        thr_ref[h] = jnp.broadcast_to(thr, (8, thr.shape[1]))


def _peer_pre(xnt, wqt, k1, k2, tt=256):
    S = xnt.shape[1]
    big = lambda: pl.BlockSpec((PEER_HEADS, PEER_NKEYS, tt), lambda i: (0, 0, i))
    sds = jax.ShapeDtypeStruct((PEER_HEADS, PEER_NKEYS, S), F32)
    return pl.pallas_call(
        _peer_pre_kernel,
        grid=(S // tt,),
        in_specs=[
            pl.BlockSpec((D_MODEL, tt), lambda i: (0, i)),
            _const_spec((PEER_HEADS * PEER_DQ, D_MODEL)),
            _const_spec((PEER_NKEYS, PEER_DQ // 2)),
            _const_spec((PEER_NKEYS, PEER_DQ // 2)),
        ],
        out_specs=[big(), big(), big(), big(),
                   pl.BlockSpec((PEER_HEADS, 8, tt), lambda i: (0, 0, i))],
        out_shape=[sds, sds, sds, sds, jax.ShapeDtypeStruct((PEER_HEADS, 8, S), F32)],
        compiler_params=_params(("parallel",)),
        name="peer_scores",
    )(xnt, wqt, k1, k2)


def _peer_kernel(xnt_ref, u_ref, vt_ref, s1_ref, s2_ref, e1_ref, e2_ref, thr_ref, h_ref, g_ref,
                 o_ref, wg_ref, acc_ref, *, ic):
    e = pl.program_id(1)

    @pl.when(e == 0)
    def _():
        acc_ref[...] = jnp.zeros(acc_ref.shape, F32)

    a = jnp.dot(u_ref[...], xnt_ref[...], preferred_element_type=F32)
    for ii in range(ic):
        rows = slice(ii * PEER_NKEYS, (ii + 1) * PEER_NKEYS)
        w = jnp.zeros((PEER_NKEYS, a.shape[1]), F32)
        for h in range(PEER_HEADS):
            hit = s1_ref[h, 0, ii:ii + 1, :] + s2_ref[h] >= thr_ref[h, 0:1, :]
            w = w + jnp.where(hit, e1_ref[h, 0, ii:ii + 1, :] * e2_ref[h], 0.0)
        wg_ref[rows, :] = (_gelu(a[rows]) * w).astype(BF16)
    acc_ref[...] += jnp.dot(vt_ref[...], wg_ref[...], preferred_element_type=F32)

    @pl.when(e == pl.num_programs(1) - 1)
    def _():
        o_ref[...] = _rms(h_ref[...] + acc_ref[...].T, g_ref[...])


def _peer(xnt, u, vt, s1, s2, e1, e2, thr, h, g, tt=512, ic=4):
    S = xnt.shape[1]
    ec = ic * PEER_NKEYS
    ne = (PEER_NKEYS * PEER_NKEYS) // ec
    tok = lambda rows: pl.BlockSpec((PEER_HEADS, rows, tt), lambda i, e: (0, 0, i))
    chunk = pl.BlockSpec((PEER_HEADS, 1, ic, tt), lambda i, e: (0, e, 0, i))
    s1 = s1.reshape(PEER_HEADS, ne, ic, S)
    e1 = e1.reshape(PEER_HEADS, ne, ic, S)
    return pl.pallas_call(
        functools.partial(_peer_kernel, ic=ic),
        grid=(S // tt, ne),
        in_specs=[
            pl.BlockSpec((D_MODEL, tt), lambda i, e: (0, i)),
            pl.BlockSpec((ec, D_MODEL), lambda i, e: (e, 0)),
            pl.BlockSpec((D_MODEL, ec), lambda i, e: (0, e)),
            chunk,
            tok(PEER_NKEYS),
            chunk,
            tok(PEER_NKEYS),
            tok(8),
            pl.BlockSpec((tt, D_MODEL), lambda i, e: (i, 0)),
            pl.BlockSpec((1, D_MODEL), lambda i, e: (0, 0)),
        ],
        out_specs=pl.BlockSpec((tt, D_MODEL), lambda i, e: (i, 0)),
        out_shape=jax.ShapeDtypeStruct((S, D_MODEL), F32),
        scratch_shapes=[pltpu.VMEM((ec, tt), BF16), pltpu.VMEM((D_MODEL, tt), F32)],
        compiler_params=_params(("parallel", "arbitrary")),
        name="peer_experts",
    )(xnt, u, vt, s1, s2, e1, e2, thr, h, g)


def kernel(x, mem, positions, norm_mix_g, w_in, da_lambda_q1, da_lambda_k1, da_lambda_q2, da_lambda_k2, da_subln_g, nsa_cmp_pos, nsa_cmpk_w1, nsa_cmpk_w2, nsa_cmpv_w1, nsa_cmpv_w2, w_branch_da, w_branch_nsa, w_out, norm_x_g, norm_mem_g, w_xq, w_xkv, w_xo, norm_ffn_g, peer_wq, peer_keys1, peer_keys2, peer_u, peer_v, norm_final_g):
    B, S, D = x.shape
    assert B == 1 and D == D_MODEL and norm_mix_g.shape[0] == 1
    l = 0
    lambda_init = 0.8 - 0.6 * math.exp(-0.3 * l)
    lam = (jnp.exp(jnp.sum((da_lambda_q1[l] * da_lambda_k1[l]).astype(F32)))
           - jnp.exp(jnp.sum((da_lambda_q2[l] * da_lambda_k2[l]).astype(F32))) + lambda_init)
    lam_row = jnp.full((1, LANES), lam, F32)

    z = _inproj(x[0], norm_mix_g[l][None], _pack_w_in(w_in[l]), *_rope_tables(positions[0]))
    o_da = _diff_attention(z, lam_row, da_subln_g[l][None], lambda_init)
    kvc = _compress(z, nsa_cmp_pos[l], jnp.stack([nsa_cmpk_w1[l], nsa_cmpv_w1[l]]),
                    jnp.stack([nsa_cmpk_w2[l], nsa_cmpv_w2[l]]))
    o_c, sel = _nsa_compressed(z, kvc)
    o_s = _nsa_selected(z, sel)
    o_w = _nsa_window(z)

    wnsa = w_branch_nsa[l].reshape(2, 2, NSA_HPG, NSA_DK, D).transpose(0, 2, 1, 3, 4)
    wnsa = wnsa.reshape(NSA_COLS, D).astype(BF16)
    mkv = _memkv(mem[0], norm_mem_g[l][None], w_xkv[l].astype(BF16))
    h, xnt = _mid(x[0], z, o_da, o_c, o_s, o_w, _gate_expand(), w_branch_da[l].astype(BF16), wnsa,
                  w_out[l].astype(BF16), norm_x_g[l][None], w_xq[l].astype(BF16), mkv,
                  w_xo[l].astype(BF16), norm_ffn_g[l][None])

    s1, s2, e1, e2, thr = _peer_pre(xnt, peer_wq[l].T.astype(BF16), peer_keys1[l].astype(BF16),
                                    peer_keys2[l].astype(BF16))
    out = _peer(xnt, peer_u[l].astype(BF16), peer_v[l].T.astype(BF16), s1, s2, e1, e2, thr, h,
                norm_final_g[None])
    return out[None]
```

```python
import functools
import math

import jax
import jax.numpy as jnp
import numpy as np
from jax import lax
from jax.experimental import pallas as pl
from jax.experimental.pallas import tpu as pltpu

F32 = jnp.float32
BF16 = jnp.bfloat16

D_MODEL = 2048
EPS = 1e-6
NEG = -1e30
ROPE_THETA = 500000.0
DA_HEADS = 8
DA_DK = 64
DA_DV = 128
NSA_GROUPS = 4
NSA_HPG = 4
NSA_DK = 64
CMP_LEN = 32
CMP_STRIDE = 16
CMP_HIDDEN = 128
SEL_LEN = 64
SEL_TOPK = 16
WINDOW = 512
FORCE_BONUS = 100.0
ROPE_DIM = 16
X_HEADS = 4
X_DH = 128
PEER_HEADS = 8
PEER_NKEYS = 128
PEER_DQ = 256
PEER_TOPK = 16

LANES = 128
VMEM_LIMIT = 56 * 1024 * 1024

C_ZQ = 0
C_ZK = 1024
C_NQ = 2048
C_KCMP = 3072
C_KSEL = 3328
C_KWIN = 3584
C_NGB = 3840
NG_LANE0 = 16
N_ROPE_COLS = 4096
C_ZV = 4096
C_VCMP = 5120
C_VSEL = 5376
C_VWIN = 5632
C_BG = 6144
N_COLS = 10240
TN_IN = 512
NSA_COLS = 1024


def _params(sem):
    return pltpu.CompilerParams(dimension_semantics=sem, vmem_limit_bytes=VMEM_LIMIT)


def _const_spec(shape):
    n = len(shape)
    return pl.BlockSpec(shape, lambda *_: (0,) * n, pipeline_mode=pl.Buffered(1))


def _gelu(x):
    return 0.5 * x * (1.0 + lax.erf(x * (2.0 ** -0.5)))


def _rms(x, g):
    ms = jnp.mean(x * x, axis=-1, keepdims=True)
    return x * lax.rsqrt(ms + EPS) * g


def _inproj_kernel(x_ref, g_ref, w_ref, c_ref, sp_ref, sm_ref, o_ref, xn_ref):
    j = pl.program_id(1)

    @pl.when(j == 0)
    def _():
        xn_ref[...] = _rms(x_ref[...], g_ref[...]).astype(BF16)

    z = jnp.dot(xn_ref[...], w_ref[...], preferred_element_type=F32)

    @pl.when(j < N_ROPE_COLS // TN_IN)
    def _():
        c = c_ref[...]
        sp = sp_ref[...]
        sm = sm_ref[...]
        for k in range(TN_IN // LANES):
            zc = z[:, k * LANES:(k + 1) * LANES]
            r = zc * c + pltpu.roll(zc, 8, 1) * sp + pltpu.roll(zc, LANES - 8, 1) * sm
            o_ref[:, k * LANES:(k + 1) * LANES] = r.astype(o_ref.dtype)

    @pl.when(j >= N_ROPE_COLS // TN_IN)
    def _():
        o_ref[...] = z.astype(o_ref.dtype)


def _inproj(x, g, w, rope_c, rope_sp, rope_sm, tm=512):
    S = x.shape[0]
    return pl.pallas_call(
        _inproj_kernel,
        grid=(S // tm, N_COLS // TN_IN),
        in_specs=[
            pl.BlockSpec((tm, D_MODEL), lambda i, j: (i, 0)),
            pl.BlockSpec((1, D_MODEL), lambda i, j: (0, 0)),
            pl.BlockSpec((D_MODEL, TN_IN), lambda i, j: (0, j)),
            pl.BlockSpec((tm, LANES), lambda i, j: (i, 0)),
            pl.BlockSpec((tm, LANES), lambda i, j: (i, 0)),
            pl.BlockSpec((tm, LANES), lambda i, j: (i, 0)),
        ],
        out_specs=pl.BlockSpec((tm, TN_IN), lambda i, j: (i, j)),
        out_shape=jax.ShapeDtypeStruct((S, N_COLS), BF16),
        scratch_shapes=[pltpu.VMEM((tm, D_MODEL), BF16)],
        compiler_params=_params(("parallel", "arbitrary")),
        name="inproj",
    )(x, g, w, rope_c, rope_sp, rope_sm)


def _pack_w_in(w_in):
    zq, zk, zv, nq, nkv, ng, bg = jnp.split(
        w_in, np.cumsum([1024, 1024, 1024, 1024, 1536, 48]).tolist(), axis=1)
    D = w_in.shape[0]
    nq = nq.reshape(D, 2, 2, NSA_HPG, NSA_DK).transpose(0, 1, 3, 2, 4).reshape(D, 1024)
    nkv = nkv.reshape(D, 6, 256)
    ngb = jnp.concatenate([jnp.zeros((D, NG_LANE0), w_in.dtype), ng,
                           jnp.zeros((D, 256 - NG_LANE0 - 48), w_in.dtype)], axis=1)
    pad = jnp.zeros((D, C_BG - C_VWIN - 256), w_in.dtype)
    w = jnp.concatenate([zq, zk, nq, nkv[:, 0], nkv[:, 2], nkv[:, 4], ngb,
                         zv, nkv[:, 1], nkv[:, 3], nkv[:, 5], pad, bg], axis=1)
    return w.astype(BF16)


def _rope_tables(positions):
    inv = ROPE_THETA ** (-jnp.arange(0, ROPE_DIM, 2, dtype=F32) / ROPE_DIM)
    ang = positions.astype(F32)[:, None] * inv
    cos, sin = jnp.cos(ang), jnp.sin(ang)
    S = positions.shape[0]
    one = jnp.ones((S, 48), F32)
    zero = jnp.zeros((S, 48), F32)
    z8 = jnp.zeros((S, 8), F32)
    c64 = jnp.concatenate([cos, cos, one], axis=1)
    sp64 = jnp.concatenate([z8, sin, zero], axis=1)
    sm64 = jnp.concatenate([-sin, z8, zero], axis=1)
    return (jnp.tile(c64, (1, 2)), jnp.tile(sp64, (1, 2)), jnp.tile(sm64, (1, 2)))


def _online_softmax_step(s, v, m_ref, l_ref, acc_ref):
    m_prev = m_ref[...]
    m_new = jnp.maximum(m_prev, jnp.max(s, axis=-1, keepdims=True))
    alpha = jnp.exp(m_prev - m_new)
    p = jnp.exp(s - m_new[:, 0:1])
    l_ref[...] = alpha * l_ref[...] + jnp.sum(p, axis=-1, keepdims=True)
    acc_ref[...] = alpha * acc_ref[...] + jnp.dot(p.astype(BF16), v, preferred_element_type=F32)
    m_ref[...] = m_new


def _da_kernel(lam_ref, q_ref, k_ref, v_ref, g_ref, o_ref, qs_ref, m_ref, l_ref, acc_ref,
               *, tq, out_scale):
    qi = pl.program_id(1)
    q = q_ref[...].astype(F32) * (DA_DK ** -0.5)
    lane = lax.broadcasted_iota(jnp.int32, q.shape, 1)
    qs_ref[0:tq, :] = jnp.where(lane < DA_DK, q, 0.0).astype(BF16)
    qs_ref[tq:2 * tq, :] = jnp.where(lane >= DA_DK, q, 0.0).astype(BF16)
    m_ref[...] = jnp.full(m_ref.shape, NEG, F32)
    l_ref[...] = jnp.zeros(l_ref.shape, F32)
    acc_ref[...] = jnp.zeros(acc_ref.shape, F32)

    def scores(ki):
        off = pl.multiple_of(ki * tq, tq)
        s = lax.dot_general(qs_ref[...], k_ref[pl.ds(off, tq), :], (((1,), (1,)), ((), ())),
                            preferred_element_type=F32)
        return s, v_ref[pl.ds(off, tq), :]

    def body(ki, carry):
        s, v = scores(ki)
        _online_softmax_step(s, v, m_ref, l_ref, acc_ref)
        return carry

    lax.fori_loop(0, qi, body, 0)
    s, v = scores(qi)
    row = lax.broadcasted_iota(jnp.int32, (tq, tq), 0)
    col = lax.broadcasted_iota(jnp.int32, (tq, tq), 1)
    s = jnp.where((col <= row)[None], s.reshape(2, tq, tq), NEG).reshape(2 * tq, tq)
    _online_softmax_step(s, v, m_ref, l_ref, acc_ref)

    o = acc_ref[...] / l_ref[...]
    d = o[0:tq] - lam_ref[...] * o[tq:2 * tq]
    o_ref[...] = (_rms(d, g_ref[...]) * out_scale).astype(o_ref.dtype)


def _diff_attention(z, lam_row, subln_g, lambda_init, tq=512):
    S = z.shape[0]
    return pl.pallas_call(
        functools.partial(_da_kernel, tq=tq, out_scale=1.0 - lambda_init),
        grid=(DA_HEADS, S // tq),
        in_specs=[
            pl.BlockSpec((1, LANES), lambda h, qi: (0, 0)),
            pl.BlockSpec((tq, LANES), lambda h, qi: (qi, C_ZQ // LANES + h)),
            pl.BlockSpec((S, LANES), lambda h, qi: (0, C_ZK // LANES + h)),
            pl.BlockSpec((S, LANES), lambda h, qi: (0, C_ZV // LANES + h)),
            pl.BlockSpec((1, LANES), lambda h, qi: (0, 0)),
        ],
        out_specs=pl.BlockSpec((tq, LANES), lambda h, qi: (qi, h)),
        out_shape=jax.ShapeDtypeStruct((S, DA_HEADS * DA_DV), BF16),
        scratch_shapes=[
            pltpu.VMEM((2 * tq, LANES), BF16),
            pltpu.VMEM((2 * tq, LANES), F32),
            pltpu.VMEM((2 * tq, LANES), F32),
            pltpu.VMEM((2 * tq, LANES), F32),
        ],
        compiler_params=_params(("parallel", "arbitrary")),
        name="diff_attn",
    )(lam_row, z, z, z, subln_g)


def _compress_kernel(x_ref, pos_ref, w1_ref, w2_ref, o_ref):
    n = x_ref.shape[3]
    w1a = w1_ref[0, 0:CMP_STRIDE * NSA_DK, :]
    w1b = w1_ref[0, CMP_STRIDE * NSA_DK:, :]
    c = jnp.dot(pos_ref[...], w1_ref[0], preferred_element_type=F32)[0:1]
    outs = []
    for gg in range(2):
        xs = x_ref[0, 0, gg]
        y1 = jnp.dot(xs, w1a, preferred_element_type=F32)
        y2 = jnp.dot(xs, w1b, preferred_element_type=F32)
        hid = _gelu(y1 + pltpu.roll(y2, n - 1, 0) + c)
        outs.append(jnp.dot(hid.astype(BF16), w2_ref[0], preferred_element_type=F32))
    o_ref[0] = jnp.concatenate(outs, axis=1).astype(o_ref.dtype)


def _compress(z, pos, w1, w2):
    S = z.shape[0]
    n = S // CMP_STRIDE

    def blocks(c0):
        a = z[:, c0:c0 + 256].reshape(n, CMP_STRIDE, 2, 2, NSA_DK)
        return a.transpose(2, 3, 0, 1, 4).reshape(2, 2, n, CMP_STRIDE * NSA_DK)

    xs = jnp.stack([blocks(C_KCMP), blocks(C_VCMP)])
    pos8 = jnp.broadcast_to(pos.reshape(1, CMP_LEN * NSA_DK), (8, CMP_LEN * NSA_DK)).astype(BF16)
    return pl.pallas_call(
        _compress_kernel,
        grid=(2, 2),
        in_specs=[
            pl.BlockSpec((1, 1, 2, n, CMP_STRIDE * NSA_DK), lambda c, p: (c, p, 0, 0, 0)),
            pl.BlockSpec((8, CMP_LEN * NSA_DK), lambda c, p: (0, 0)),
            pl.BlockSpec((1, CMP_LEN * NSA_DK, CMP_HIDDEN), lambda c, p: (c, 0, 0)),
            pl.BlockSpec((1, CMP_HIDDEN, NSA_DK), lambda c, p: (c, 0, 0)),
        ],
        out_specs=pl.BlockSpec((1, n, LANES), lambda c, p: (c, 0, p)),
        out_shape=jax.ShapeDtypeStruct((2, n, 256), BF16),
        compiler_params=_params(("parallel", "parallel")),
        name="nsa_compress",
    )(xs, pos8, w1.astype(BF16), w2.astype(BF16))


def _stack_group_queries(q_ref, qs_ref, tq):
    lane = lax.broadcasted_iota(jnp.int32, (tq, LANES), 1)
    for h in range(NSA_HPG):
        q = q_ref[:, h * LANES:(h + 1) * LANES].astype(F32) * (NSA_DK ** -0.5)
        qs_ref[0, h * tq:(h + 1) * tq, :] = jnp.where(lane < NSA_DK, q, 0.0).astype(BF16)
        qs_ref[1, h * tq:(h + 1) * tq, :] = jnp.where(lane >= NSA_DK, q, 0.0).astype(BF16)


def _merge_pair_heads(o0, o1, o_ref, tq):
    lane = lax.broadcasted_iota(jnp.int32, (tq, LANES), 1)
    for h in range(NSA_HPG):
        o_ref[:, h * LANES:(h + 1) * LANES] = jnp.where(
            lane < NSA_DK, o0[h * tq:(h + 1) * tq], o1[h * tq:(h + 1) * tq]).astype(o_ref.dtype)


def _cmp_kernel(q_ref, kc_ref, vc_ref, pool_ref, o_ref, sel_ref, qs_ref, *, tq):
    qi = pl.program_id(1)
    n = kc_ref.shape[1]
    _stack_group_queries(q_ref, qs_ref, tq)
    t = qi * tq + lax.broadcasted_iota(jnp.int32, (tq, n), 0)
    cmp_end = lax.broadcasted_iota(jnp.int32, (tq, n), 1) * CMP_STRIDE + (CMP_LEN - 1)
    vis = (cmp_end <= t)[None]
    tb = qi * tq + lax.broadcasted_iota(jnp.int32, (tq, LANES), 0)
    blk = lax.broadcasted_iota(jnp.int32, (tq, LANES), 1)
    elig = blk * SEL_LEN <= tb
    cur = tb // SEL_LEN
    forced = (blk == 0) | (blk == cur) | (blk == cur - 1)
    outs, scores = [], []
    for gg in range(2):
        s = lax.dot_general(qs_ref[gg], kc_ref[0], (((1,), (1,)), ((), ())),
                            preferred_element_type=F32).reshape(NSA_HPG, tq, n)
        s = jnp.where(vis, s, NEG)
        m = jnp.max(s, axis=-1, keepdims=True)
        e = jnp.where(vis, jnp.exp(s - m), 0.0)
        den = jnp.sum(e, axis=-1, keepdims=True)
        p = e / jnp.where(den == 0.0, 1.0, den)
        outs.append(jnp.dot(p.reshape(NSA_HPG * tq, n).astype(BF16), vc_ref[0],
                            preferred_element_type=F32))
        imp = jnp.sum(p, axis=0)
        hi = imp.astype(BF16)
        r1 = imp - hi.astype(F32)
        mid = r1.astype(BF16)
        lo = (r1 - mid.astype(F32)).astype(BF16)
        pooled = (jnp.dot(hi, pool_ref[...], preferred_element_type=F32)
                  + jnp.dot(mid, pool_ref[...], preferred_element_type=F32)
                  + jnp.dot(lo, pool_ref[...], preferred_element_type=F32))
        scores.append(jnp.where(elig, pooled + jnp.where(forced, FORCE_BONUS, 0.0), -1.0))
    sels = [jnp.zeros((tq, LANES), F32), jnp.zeros((tq, LANES), F32)]
    for _ in range(SEL_TOPK):
        for gg in range(2):
            hit = blk == jnp.argmax(scores[gg], axis=-1, keepdims=True)
            sels[gg] = jnp.where(hit & elig, 1.0, sels[gg])
            scores[gg] = jnp.where(hit, -3e38, scores[gg])
    for gg in range(2):
        sel_ref[:, gg * LANES:(gg + 1) * LANES] = sels[gg].astype(sel_ref.dtype)
    _merge_pair_heads(outs[0], outs[1], o_ref, tq)


def _nsa_compressed(z, kvc, tq=256):
    S = z.shape[0]
    n = S // CMP_STRIDE
    pool = (np.arange(n)[:, None] // (SEL_LEN // CMP_STRIDE) == np.arange(LANES)[None, :])
    pool = jnp.asarray(pool, BF16)
    return pl.pallas_call(
        functools.partial(_cmp_kernel, tq=tq),
        grid=(2, S // tq),
        in_specs=[
            pl.BlockSpec((tq, 512), lambda p, qi: (qi, C_NQ // 512 + p)),
            pl.BlockSpec((1, n, LANES), lambda p, qi: (0, 0, p)),
            pl.BlockSpec((1, n, LANES), lambda p, qi: (1, 0, p)),
            pl.BlockSpec((n, LANES), lambda p, qi: (0, 0)),
        ],
        out_specs=[
            pl.BlockSpec((tq, 512), lambda p, qi: (qi, p)),
            pl.BlockSpec((tq, 256), lambda p, qi: (qi, p)),
        ],
        out_shape=[
            jax.ShapeDtypeStruct((S, NSA_COLS), BF16),
            jax.ShapeDtypeStruct((S, 512), BF16),
        ],
        scratch_shapes=[pltpu.VMEM((2, NSA_HPG * tq, LANES), BF16)],
        compiler_params=_params(("parallel", "parallel")),
        name="nsa_cmp_topk",
    )(z, kvc, kvc, pool)


def _sel_kernel(q_ref, k_ref, v_ref, sel_ref, ex_ref, o_ref, qs_ref, m_ref, l_ref, acc_ref,
                *, tq, tk):
    qi = pl.program_id(1)
    _stack_group_queries(q_ref, qs_ref, tq)
    m_ref[...] = jnp.full(m_ref.shape, NEG, F32)
    l_ref[...] = jnp.zeros(l_ref.shape, F32)
    acc_ref[...] = jnp.zeros(acc_ref.shape, F32)
    row = qi * tq + lax.broadcasted_iota(jnp.int32, (tq, tk), 0)
    col = lax.broadcasted_iota(jnp.int32, (tq, tk), 1)

    def body(ki, carry):
        off = pl.multiple_of(ki * tk, tk)
        k = k_ref[pl.ds(off, tk), :]
        v = v_ref[pl.ds(off, tk), :]
        causal = off + col <= row
        for gg in range(2):
            hit = jnp.dot(sel_ref[:, gg * LANES:(gg + 1) * LANES], ex_ref[ki],
                          preferred_element_type=F32)
            bias = jnp.where(causal & (hit > 0.5), 0.0, NEG)
            s = lax.dot_general(qs_ref[gg], k, (((1,), (1,)), ((), ())),
                                preferred_element_type=F32)
            s = (s.reshape(NSA_HPG, tq, tk) + bias[None]).reshape(NSA_HPG * tq, tk)
            _online_softmax_step(s, v, m_ref.at[gg], l_ref.at[gg], acc_ref.at[gg])
        return carry

    lax.fori_loop(0, (qi * tq + tq - 1) // tk + 1, body, 0)
    _merge_pair_heads(acc_ref[0] / l_ref[0], acc_ref[1] / l_ref[1], o_ref, tq)


def _nsa_selected(z, sel, tq=256, tk=512):
    S = z.shape[0]
    tk = min(tk, S)
    expand = np.arange(LANES)[:, None] == np.arange(S)[None, :] // SEL_LEN
    expand = jnp.asarray(expand.reshape(LANES, S // tk, tk).transpose(1, 0, 2), BF16)
    return pl.pallas_call(
        functools.partial(_sel_kernel, tq=tq, tk=tk),
        grid=(2, S // tq),
        in_specs=[
            pl.BlockSpec((tq, 512), lambda p, qi: (qi, C_NQ // 512 + p)),
            pl.BlockSpec((S, LANES), lambda p, qi: (0, C_KSEL // LANES + p)),
            pl.BlockSpec((S, LANES), lambda p, qi: (0, C_VSEL // LANES + p)),
            pl.BlockSpec((tq, 256), lambda p, qi: (qi, p)),
            pl.BlockSpec((S // tk, LANES, tk), lambda p, qi: (0, 0, 0)),
        ],
        out_specs=pl.BlockSpec((tq, 512), lambda p, qi: (qi, p)),
        out_shape=jax.ShapeDtypeStruct((S, NSA_COLS), BF16),
        scratch_shapes=[
            pltpu.VMEM((2, NSA_HPG * tq, LANES), BF16),
            pltpu.VMEM((2, NSA_HPG * tq, LANES), F32),
            pltpu.VMEM((2, NSA_HPG * tq, LANES), F32),
            pltpu.VMEM((2, NSA_HPG * tq, LANES), F32),
        ],
        compiler_params=_params(("parallel", "arbitrary")),
        name="nsa_selected",
    )(z, z, z, sel, expand)


def _win_kernel(q_ref, *refs, tq, nprev):
    k_refs = refs[:nprev + 1]
    v_refs = refs[nprev + 1:2 * nprev + 2]
    o_ref, qs_ref = refs[2 * nprev + 2:]
    qi = pl.program_id(1)
    nk = (nprev + 1) * tq
    _stack_group_queries(q_ref, qs_ref, tq)
    k = jnp.concatenate([r[...] for r in k_refs], axis=0)
    v = jnp.concatenate([r[...] for r in v_refs], axis=0)
    t = qi * tq + lax.broadcasted_iota(jnp.int32, (tq, nk), 0)
    kpos = (qi - nprev) * tq + lax.broadcasted_iota(jnp.int32, (tq, nk), 1)
    dist = t - kpos
    bias = jnp.where((dist >= 0) & (dist < WINDOW) & (kpos >= 0), 0.0, NEG)[None]
    outs = []
    for gg in range(2):
        s = lax.dot_general(qs_ref[gg], k, (((1,), (1,)), ((), ())), preferred_element_type=F32)
        s = s.reshape(NSA_HPG, tq, nk) + bias
        m = jnp.max(s, axis=-1, keepdims=True)
        e = jnp.exp(s - m)
        p = e / jnp.sum(e, axis=-1, keepdims=True)
        outs.append(jnp.dot(p.reshape(NSA_HPG * tq, nk).astype(BF16), v,
                            preferred_element_type=F32))
    _merge_pair_heads(outs[0], outs[1], o_ref, tq)


def _nsa_window(z, tq=256):
    S = z.shape[0]
    nprev = WINDOW // tq

    def kv_spec(off, back):
        return pl.BlockSpec((tq, LANES), lambda p, qi: (jnp.maximum(qi - back, 0), off + p))

    backs = list(range(nprev, -1, -1))
    return pl.pallas_call(
        functools.partial(_win_kernel, tq=tq, nprev=nprev),
        grid=(2, S // tq),
        in_specs=[pl.BlockSpec((tq, 512), lambda p, qi: (qi, C_NQ // 512 + p))]
        + [kv_spec(C_KWIN // LANES, b) for b in backs]
        + [kv_spec(C_VWIN // LANES, b) for b in backs],
        out_specs=pl.BlockSpec((tq, 512), lambda p, qi: (qi, p)),
        out_shape=jax.ShapeDtypeStruct((S, NSA_COLS), BF16),
        scratch_shapes=[pltpu.VMEM((2, NSA_HPG * tq, LANES), BF16)],
        compiler_params=_params(("parallel", "parallel")),
        name="nsa_window",
    )(z, *([z] * (2 * nprev + 2)))


def _memkv_kernel(mem_ref, g_ref, w_ref, o_ref):
    memn = _rms(mem_ref[...], g_ref[...]).astype(BF16)
    o_ref[...] = jnp.dot(memn, w_ref[...], preferred_element_type=F32).astype(o_ref.dtype)


def _memkv(mem, g, w):
    M = mem.shape[0]
    return pl.pallas_call(
        _memkv_kernel,
        out_shape=jax.ShapeDtypeStruct((M, 2 * X_HEADS * X_DH), BF16),
        compiler_params=pltpu.CompilerParams(vmem_limit_bytes=VMEM_LIMIT),
        name="mem_kv",
    )(mem, g, w)


def _mid_kernel(x_ref, oda_ref, oc_ref, os_ref, ow_ref, ng_ref, bg0_ref, bg1_ref, eg_ref,
                wda_ref, wnsa_ref, wout_ref, gx_ref, wxq_ref, mkv_ref, wxo_ref, gf_ref,
                h_ref, xnt_ref):
    ng = ng_ref[...]
    o_nsa = jnp.zeros(oc_ref.shape, F32)
    for c, br in enumerate((oc_ref, os_ref, ow_ref)):
        gate = jax.nn.sigmoid(jnp.dot(ng, eg_ref[c], preferred_element_type=F32))
        o_nsa = o_nsa + gate * br[...].astype(F32)
    y_da = jnp.dot(oda_ref[...], wda_ref[...], preferred_element_type=F32)
    y_nsa = jnp.dot(o_nsa.astype(BF16), wnsa_ref[...], preferred_element_type=F32)
    mix = (jax.nn.sigmoid(bg0_ref[...].astype(F32)) * y_da
           + jax.nn.sigmoid(bg1_ref[...].astype(F32)) * y_nsa)
    h = x_ref[...] + jnp.dot(mix.astype(BF16), wout_ref[...], preferred_element_type=F32)
    hn = _rms(h, gx_ref[...]).astype(BF16)
    q = (jnp.dot(hn, wxq_ref[...], preferred_element_type=F32) * (X_DH ** -0.5)).astype(BF16)
    outs = []
    for hd in range(X_HEADS):
        k = mkv_ref[:, hd * X_DH:(hd + 1) * X_DH]
        v = mkv_ref[:, (X_HEADS + hd) * X_DH:(X_HEADS + hd + 1) * X_DH]
        s = lax.dot_general(q[:, hd * X_DH:(hd + 1) * X_DH], k, (((1,), (1,)), ((), ())),
                            preferred_element_type=F32)
        e = jnp.exp(s - jnp.max(s, axis=-1, keepdims=True))
        p = (e / jnp.sum(e, axis=-1, keepdims=True)).astype(BF16)
        outs.append(jnp.dot(p, v, preferred_element_type=F32))
    o = jnp.concatenate(outs, axis=1).astype(BF16)
    h = h + jnp.dot(o, wxo_ref[...], preferred_element_type=F32)
    h_ref[...] = h
    xnt_ref[...] = _rms(h, gf_ref[...]).T.astype(BF16)


def _mid(x, z, o_da, o_c, o_s, o_w, eg, wda, wnsa, wout, gx, wxq, mkv, wxo, gf, tm=256):
    S = x.shape[0]
    M = mkv.shape[0]
    row = lambda w: pl.BlockSpec((tm, w), lambda i: (i, 0))
    return pl.pallas_call(
        _mid_kernel,
        grid=(S // tm,),
        in_specs=[
            row(D_MODEL), row(1024), row(NSA_COLS), row(NSA_COLS), row(NSA_COLS),
            pl.BlockSpec((tm, LANES), lambda i: (i, C_NGB // LANES)),
            pl.BlockSpec((tm, D_MODEL), lambda i: (i, C_BG // D_MODEL)),
            pl.BlockSpec((tm, D_MODEL), lambda i: (i, C_BG // D_MODEL + 1)),
            _const_spec((3, LANES, NSA_COLS)),
            _const_spec((1024, D_MODEL)), _const_spec((NSA_COLS, D_MODEL)),
            _const_spec((D_MODEL, D_MODEL)),
            _const_spec((1, D_MODEL)), _const_spec((D_MODEL, X_HEADS * X_DH)),
            _const_spec((M, 2 * X_HEADS * X_DH)), _const_spec((X_HEADS * X_DH, D_MODEL)),
            _const_spec((1, D_MODEL)),
        ],
        out_specs=[
            pl.BlockSpec((tm, D_MODEL), lambda i: (i, 0)),
            pl.BlockSpec((D_MODEL, tm), lambda i: (0, i)),
        ],
        out_shape=[
            jax.ShapeDtypeStruct((S, D_MODEL), F32),
            jax.ShapeDtypeStruct((D_MODEL, S), BF16),
        ],
        compiler_params=_params(("parallel",)),
        name="merge_xattn",
    )(x, o_da, o_c, o_s, o_w, z, z, z, eg, wda, wnsa, wout, gx, wxq, mkv, wxo, gf)


def _gate_expand():
    e = np.zeros((3, LANES, NSA_COLS), np.float32)
    for c in range(3):
        for g in range(NSA_GROUPS):
            for h in range(NSA_HPG):
                col = ((g // 2) * NSA_HPG + h) * LANES + (g % 2) * NSA_DK
                e[c, NG_LANE0 + g * 12 + h * 3 + c, col:col + NSA_DK] = 1.0
    return jnp.asarray(e, BF16)


def _topk_rows(x, k):
    vals = []
    for _ in range(k):
        m = jnp.max(x, axis=0, keepdims=True)
        vals.append(m)
        x = jnp.where(x == m, -3e38, x)
    return vals


def _peer_pre_kernel(xnt_ref, wqt_ref, k1_ref, k2_ref, t1_ref, s2_ref, e1_ref, e2_ref):
    half = PEER_DQ // 2
    qt = jnp.dot(wqt_ref[...], xnt_ref[...], preferred_element_type=F32).astype(BF16)
    for h in range(PEER_HEADS):
        s1 = jnp.dot(k1_ref[...], qt[h * PEER_DQ:h * PEER_DQ + half], preferred_element_type=F32)
        s2 = jnp.dot(k2_ref[...], qt[h * PEER_DQ + half:(h + 1) * PEER_DQ],
                     preferred_element_type=F32)
        v1 = _topk_rows(s1, PEER_TOPK)
        v2 = jnp.concatenate(_topk_rows(s2, PEER_TOPK), axis=0)
        cand = jnp.concatenate([a + v2 for a in v1], axis=0)
        top = _topk_rows(cand, PEER_TOPK + 1)
        thr = 0.5 * (top[PEER_TOPK - 1] + top[PEER_TOPK])
        zsum = jnp.sum(jnp.where(cand > thr, jnp.exp(cand - top[0]), 0.0), axis=0, keepdims=True)
        t1_ref[h] = thr - s1
        s2_ref[h] = s2
        e1_ref[h] = jnp.exp(s1 - v1[0])
        e2_ref[h] = jnp.exp(s2 - v2[0:1]) / zsum


def _peer_pre(xnt, wqt, k1, k2, tt=256):
    S = xnt.shape[1]
    big = lambda: pl.BlockSpec((PEER_HEADS, PEER_NKEYS, tt), lambda i: (0, 0, i))
    sds = jax.ShapeDtypeStruct((PEER_HEADS, PEER_NKEYS, S), F32)
    return pl.pallas_call(
        _peer_pre_kernel,
        grid=(S // tt,),
        in_specs=[
            pl.BlockSpec((D_MODEL, tt), lambda i: (0, i)),
            _const_spec((PEER_HEADS * PEER_DQ, D_MODEL)),
            _const_spec((PEER_NKEYS, PEER_DQ // 2)),
            _const_spec((PEER_NKEYS, PEER_DQ // 2)),
        ],
        out_specs=[big(), big(), big(), big()],
        out_shape=[sds, sds, sds, sds],
        compiler_params=_params(("parallel",)),
        name="peer_scores",
    )(xnt, wqt, k1, k2)


PEER_JB = 32


def _peer_kernel(xnt_ref, u_ref, vt_ref, t1_ref, s2_ref, e1_ref, e2_ref, o_ref, a_ref, wg_ref,
                 wgp_ref, *, ic):
    e = pl.program_id(1)
    ne = pl.num_programs(1) - 1
    tt = xnt_ref.shape[1]

    @pl.when(e == 0)
    def _():
        o_ref[...] = jnp.zeros(o_ref.shape, F32)
        wg_ref[...] = jnp.zeros(wg_ref.shape, BF16)

    def apply_previous():
        wgp_ref[...] = wg_ref[...]
        o_ref[...] += jnp.dot(vt_ref[...], wgp_ref[...], preferred_element_type=F32)

    @pl.when(e < ne)
    def _():
        a_ref[...] = jnp.dot(u_ref[...], xnt_ref[...], preferred_element_type=F32)
        apply_previous()
        for ii in range(ic):
            for c in range(tt // LANES):
                cols = slice(c * LANES, (c + 1) * LANES)
                taus = [t1_ref[h, 0, ii:ii + 1, cols] for h in range(PEER_HEADS)]
                e1s = [e1_ref[h, 0, ii:ii + 1, cols] for h in range(PEER_HEADS)]
                for jb in range(PEER_NKEYS // PEER_JB):
                    j = slice(jb * PEER_JB, (jb + 1) * PEER_JB)
                    rows = slice(ii * PEER_NKEYS + jb * PEER_JB,
                                 ii * PEER_NKEYS + (jb + 1) * PEER_JB)
                    w = jnp.zeros((PEER_JB, LANES), F32)
                    for h in range(PEER_HEADS):
                        w = w + jnp.where(s2_ref[h, j, cols] > taus[h],
                                          e2_ref[h, j, cols] * e1s[h], 0.0)
                    wg_ref[rows, cols] = (_gelu(a_ref[rows, cols]) * w).astype(BF16)

    @pl.when(e == ne)
    def _():
        apply_previous()


def _peer(xnt, u, vt, t1, s2, e1, e2, tt=512, ic=8):
    S = xnt.shape[1]
    ec = ic * PEER_NKEYS
    ne = (PEER_NKEYS * PEER_NKEYS) // ec
    tok = pl.BlockSpec((PEER_HEADS, PEER_NKEYS, tt), lambda i, e: (0, 0, i))
    chunk = pl.BlockSpec((PEER_HEADS, 1, ic, tt), lambda i, e: (0, jnp.minimum(e, ne - 1), 0, i))
    t1 = t1.reshape(PEER_HEADS, ne, ic, S)
    e1 = e1.reshape(PEER_HEADS, ne, ic, S)
    return pl.pallas_call(
        functools.partial(_peer_kernel, ic=ic),
        grid=(S // tt, ne + 1),
        in_specs=[
            pl.BlockSpec((D_MODEL, tt), lambda i, e: (0, i)),
            pl.BlockSpec((ec, D_MODEL), lambda i, e: (jnp.minimum(e, ne - 1), 0)),
            pl.BlockSpec((D_MODEL, ec), lambda i, e: (0, jnp.maximum(e - 1, 0))),
            chunk, tok, chunk, tok,
        ],
        out_specs=pl.BlockSpec((D_MODEL, tt), lambda i, e: (0, i)),
        out_shape=jax.ShapeDtypeStruct((D_MODEL, S), F32),
        scratch_shapes=[pltpu.VMEM((ec, tt), F32), pltpu.VMEM((ec, tt), BF16),
                        pltpu.VMEM((ec, tt), BF16)],
        compiler_params=_params(("parallel", "arbitrary")),
        name="peer_experts",
    )(xnt, u, vt, t1, s2, e1, e2)


def _final_kernel(h_ref, yt_ref, g_ref, o_ref):
    o_ref[...] = _rms(h_ref[...] + yt_ref[...].T, g_ref[...])


def _final(h, yt, g, tm=256):
    S = h.shape[0]
    return pl.pallas_call(
        _final_kernel,
        grid=(S // tm,),
        in_specs=[
            pl.BlockSpec((tm, D_MODEL), lambda i: (i, 0)),
            pl.BlockSpec((D_MODEL, tm), lambda i: (0, i)),
            pl.BlockSpec((1, D_MODEL), lambda i: (0, 0)),
        ],
        out_specs=pl.BlockSpec((tm, D_MODEL), lambda i: (i, 0)),
        out_shape=jax.ShapeDtypeStruct((S, D_MODEL), F32),
        compiler_params=_params(("parallel",)),
        name="final_norm",
    )(h, yt, g)


def kernel(x, mem, positions, norm_mix_g, w_in, da_lambda_q1, da_lambda_k1, da_lambda_q2, da_lambda_k2, da_subln_g, nsa_cmp_pos, nsa_cmpk_w1, nsa_cmpk_w2, nsa_cmpv_w1, nsa_cmpv_w2, w_branch_da, w_branch_nsa, w_out, norm_x_g, norm_mem_g, w_xq, w_xkv, w_xo, norm_ffn_g, peer_wq, peer_keys1, peer_keys2, peer_u, peer_v, norm_final_g):
    B, S, D = x.shape
    assert B == 1 and D == D_MODEL and norm_mix_g.shape[0] == 1
    l = 0
    lambda_init = 0.8 - 0.6 * math.exp(-0.3 * l)
    lam = (jnp.exp(jnp.sum((da_lambda_q1[l] * da_lambda_k1[l]).astype(F32)))
           - jnp.exp(jnp.sum((da_lambda_q2[l] * da_lambda_k2[l]).astype(F32))) + lambda_init)
    lam_row = jnp.full((1, LANES), lam, F32)

    z = _inproj(x[0], norm_mix_g[l][None], _pack_w_in(w_in[l]), *_rope_tables(positions[0]))
    o_da = _diff_attention(z, lam_row, da_subln_g[l][None], lambda_init)
    kvc = _compress(z, nsa_cmp_pos[l], jnp.stack([nsa_cmpk_w1[l], nsa_cmpv_w1[l]]),
                    jnp.stack([nsa_cmpk_w2[l], nsa_cmpv_w2[l]]))
    o_c, sel = _nsa_compressed(z, kvc)
    o_s = _nsa_selected(z, sel)
    o_w = _nsa_window(z)

    wnsa = w_branch_nsa[l].reshape(2, 2, NSA_HPG, NSA_DK, D).transpose(0, 2, 1, 3, 4)
    wnsa = wnsa.reshape(NSA_COLS, D).astype(BF16)
    mkv = _memkv(mem[0], norm_mem_g[l][None], w_xkv[l].astype(BF16))
    h, xnt = _mid(x[0], z, o_da, o_c, o_s, o_w, _gate_expand(), w_branch_da[l].astype(BF16), wnsa,
                  w_out[l].astype(BF16), norm_x_g[l][None], w_xq[l].astype(BF16), mkv,
                  w_xo[l].astype(BF16), norm_ffn_g[l][None])

    t1, s2, e1, e2 = _peer_pre(xnt, peer_wq[l].T.astype(BF16), peer_keys1[l].astype(BF16),
                               peer_keys2[l].astype(BF16))
    yt = _peer(xnt, peer_u[l].astype(BF16), peer_v[l].T.astype(BF16), t1, s2, e1, e2)
    return _final(h, yt, norm_final_g[None])[None]
```

```python
import functools
import math

import jax
import jax.numpy as jnp
import numpy as np
from jax import lax
from jax.experimental import pallas as pl
from jax.experimental.pallas import tpu as pltpu

F32 = jnp.float32
BF16 = jnp.bfloat16

D_MODEL = 2048
EPS = 1e-6
NEG = -1e30
ROPE_THETA = 500000.0
DA_HEADS = 8
DA_DK = 64
DA_DV = 128
NSA_GROUPS = 4
NSA_HPG = 4
NSA_DK = 64
CMP_LEN = 32
CMP_STRIDE = 16
CMP_HIDDEN = 128
SEL_LEN = 64
SEL_TOPK = 16
WINDOW = 512
FORCE_BONUS = 100.0
ROPE_DIM = 16
X_HEADS = 4
X_DH = 128
PEER_HEADS = 8
PEER_NKEYS = 128
PEER_DQ = 256
PEER_TOPK = 16

LANES = 128
VMEM_LIMIT = 56 * 1024 * 1024

C_ZQ = 0
C_ZK = 1024
C_NQ = 2048
C_KCMP = 3072
C_KSEL = 3328
C_KWIN = 3584
C_NGB = 3840
NG_LANE0 = 16
N_ROPE_COLS = 4096
C_ZV = 4096
C_VCMP = 5120
C_VSEL = 5376
C_VWIN = 5632
C_BG = 6144
N_COLS = 10240
TN_IN = 1024
NSA_COLS = 1024


def _params(sem):
    return pltpu.CompilerParams(dimension_semantics=sem, vmem_limit_bytes=VMEM_LIMIT)


def _const_spec(shape):
    n = len(shape)
    return pl.BlockSpec(shape, lambda *_: (0,) * n, pipeline_mode=pl.Buffered(1))


def _gelu(x):
    return 0.5 * x * (1.0 + lax.erf(x * (2.0 ** -0.5)))


def _rms(x, g):
    ms = jnp.mean(x * x, axis=-1, keepdims=True)
    return x * lax.rsqrt(ms + EPS) * g


def _inproj_kernel(x_ref, g_ref, w_ref, c_ref, sp_ref, sm_ref, o_ref, xn_ref):
    j = pl.program_id(1)

    @pl.when(j == 0)
    def _():
        xn_ref[...] = _rms(x_ref[...], g_ref[...]).astype(BF16)

    z = jnp.dot(xn_ref[...], w_ref[...], preferred_element_type=F32)

    @pl.when(j < N_ROPE_COLS // TN_IN)
    def _():
        c = c_ref[...]
        sp = sp_ref[...]
        sm = sm_ref[...]
        for k in range(TN_IN // LANES):
            zc = z[:, k * LANES:(k + 1) * LANES]
            r = zc * c + pltpu.roll(zc, 8, 1) * sp + pltpu.roll(zc, LANES - 8, 1) * sm
            o_ref[:, k * LANES:(k + 1) * LANES] = r.astype(o_ref.dtype)

    @pl.when(j >= N_ROPE_COLS // TN_IN)
    def _():
        o_ref[...] = z.astype(o_ref.dtype)


def _inproj(x, g, w, rope_c, rope_sp, rope_sm, tm=512):
    S = x.shape[0]
    return pl.pallas_call(
        _inproj_kernel,
        grid=(S // tm, N_COLS // TN_IN),
        in_specs=[
            pl.BlockSpec((tm, D_MODEL), lambda i, j: (i, 0)),
            pl.BlockSpec((1, D_MODEL), lambda i, j: (0, 0)),
            pl.BlockSpec((D_MODEL, TN_IN), lambda i, j: (0, j)),
            pl.BlockSpec((tm, LANES), lambda i, j: (i, 0)),
            pl.BlockSpec((tm, LANES), lambda i, j: (i, 0)),
            pl.BlockSpec((tm, LANES), lambda i, j: (i, 0)),
        ],
        out_specs=pl.BlockSpec((tm, TN_IN), lambda i, j: (i, j)),
        out_shape=jax.ShapeDtypeStruct((S, N_COLS), BF16),
        scratch_shapes=[pltpu.VMEM((tm, D_MODEL), BF16)],
        compiler_params=_params(("parallel", "arbitrary")),
        name="inproj",
    )(x, g, w, rope_c, rope_sp, rope_sm)


def _pack_w_in(w_in):
    zq, zk, zv, nq, nkv, ng, bg = jnp.split(
        w_in, np.cumsum([1024, 1024, 1024, 1024, 1536, 48]).tolist(), axis=1)
    D = w_in.shape[0]
    nq = nq.reshape(D, 2, 2, NSA_HPG, NSA_DK).transpose(0, 1, 3, 2, 4).reshape(D, 1024)
    nkv = nkv.reshape(D, 6, 256)
    ngb = jnp.concatenate([jnp.zeros((D, NG_LANE0), w_in.dtype), ng,
                           jnp.zeros((D, 256 - NG_LANE0 - 48), w_in.dtype)], axis=1)
    pad = jnp.zeros((D, C_BG - C_VWIN - 256), w_in.dtype)
    w = jnp.concatenate([zq, zk, nq, nkv[:, 0], nkv[:, 2], nkv[:, 4], ngb,
                         zv, nkv[:, 1], nkv[:, 3], nkv[:, 5], pad, bg], axis=1)
    return w.astype(BF16)


def _rope_tables(positions):
    inv = ROPE_THETA ** (-jnp.arange(0, ROPE_DIM, 2, dtype=F32) / ROPE_DIM)
    ang = positions.astype(F32)[:, None] * inv
    cos, sin = jnp.cos(ang), jnp.sin(ang)
    S = positions.shape[0]
    one = jnp.ones((S, 48), F32)
    zero = jnp.zeros((S, 48), F32)
    z8 = jnp.zeros((S, 8), F32)
    c64 = jnp.concatenate([cos, cos, one], axis=1)
    sp64 = jnp.concatenate([z8, sin, zero], axis=1)
    sm64 = jnp.concatenate([-sin, z8, zero], axis=1)
    return (jnp.tile(c64, (1, 2)), jnp.tile(sp64, (1, 2)), jnp.tile(sm64, (1, 2)))


def _online_softmax_step(s, v, m_ref, l_ref, acc_ref):
    m_prev = m_ref[...]
    m_new = jnp.maximum(m_prev, jnp.max(s, axis=-1, keepdims=True))
    alpha = jnp.exp(m_prev - m_new)
    p = jnp.exp(s - m_new[:, 0:1])
    l_ref[...] = alpha * l_ref[...] + jnp.sum(p, axis=-1, keepdims=True)
    acc_ref[...] = alpha * acc_ref[...] + jnp.dot(p.astype(BF16), v, preferred_element_type=F32)
    m_ref[...] = m_new


DA_SPLIT = 2
DA_WIDE = 4


def _da_kernel(lam_ref, q_ref, k_ref, v_ref, g_ref, o_ref, qs_ref, m_ref, l_ref, acc_ref,
               *, tq, out_scale):
    qi = pl.program_id(1)
    q = q_ref[...].astype(F32) * (DA_DK ** -0.5)
    lane = lax.broadcasted_iota(jnp.int32, q.shape, 1)
    qs_ref[0:tq, :] = jnp.where(lane < DA_DK, q, 0.0).astype(BF16)
    qs_ref[tq:2 * tq, :] = jnp.where(lane >= DA_DK, q, 0.0).astype(BF16)
    m_ref[...] = jnp.full(m_ref.shape, NEG, F32)
    l_ref[...] = jnp.zeros(l_ref.shape, F32)
    acc_ref[...] = jnp.zeros(acc_ref.shape, F32)

    rc = tq // DA_SPLIT

    def tile(off, width, masked):
        off = pl.multiple_of(off, tq)
        k = k_ref[pl.ds(off, width), :]
        v = v_ref[pl.ds(off, width), :]
        for c in range(2 * DA_SPLIT):
            r = slice(c * rc, (c + 1) * rc)
            s = lax.dot_general(qs_ref[r, :], k, (((1,), (1,)), ((), ())),
                                preferred_element_type=F32)
            if masked:
                row = (c % DA_SPLIT) * rc + lax.broadcasted_iota(jnp.int32, (rc, width), 0)
                col = lax.broadcasted_iota(jnp.int32, (rc, width), 1)
                s = jnp.where(col <= row, s, NEG)
            _online_softmax_step(s, v, m_ref.at[r], l_ref.at[r], acc_ref.at[r])

    def wide(kw, carry):
        tile(kw * (DA_WIDE * tq), DA_WIDE * tq, False)
        return carry

    def narrow(ki, carry):
        tile(ki * tq, tq, False)
        return carry

    nw = qi // DA_WIDE
    lax.fori_loop(0, nw, wide, 0)
    rem = qi - nw * DA_WIDE

    @pl.when(rem >= 2)
    def _():
        tile(nw * (DA_WIDE * tq), 2 * tq, False)

    lax.fori_loop(nw * DA_WIDE + 2 * (rem // 2), qi, narrow, 0)
    tile(qi * tq, tq, True)

    o = acc_ref[...] / l_ref[...]
    d = o[0:tq] - lam_ref[...] * o[tq:2 * tq]
    o_ref[...] = (_rms(d, g_ref[...]) * out_scale).astype(o_ref.dtype)


def _diff_attention(z, lam_row, subln_g, lambda_init, tq=512):
    S = z.shape[0]
    return pl.pallas_call(
        functools.partial(_da_kernel, tq=tq, out_scale=1.0 - lambda_init),
        grid=(DA_HEADS, S // tq),
        in_specs=[
            pl.BlockSpec((1, LANES), lambda h, qi: (0, 0)),
            pl.BlockSpec((tq, LANES), lambda h, qi: (qi, C_ZQ // LANES + h)),
            pl.BlockSpec((S, LANES), lambda h, qi: (0, C_ZK // LANES + h)),
            pl.BlockSpec((S, LANES), lambda h, qi: (0, C_ZV // LANES + h)),
            pl.BlockSpec((1, LANES), lambda h, qi: (0, 0)),
        ],
        out_specs=pl.BlockSpec((tq, LANES), lambda h, qi: (qi, h)),
        out_shape=jax.ShapeDtypeStruct((S, DA_HEADS * DA_DV), BF16),
        scratch_shapes=[
            pltpu.VMEM((2 * tq, LANES), BF16),
            pltpu.VMEM((2 * tq, LANES), F32),
            pltpu.VMEM((2 * tq, LANES), F32),
            pltpu.VMEM((2 * tq, LANES), F32),
        ],
        compiler_params=_params(("parallel", "arbitrary")),
        name="diff_attn",
    )(lam_row, z, z, z, subln_g)


def _compress_kernel(x_ref, pos_ref, w1_ref, w2_ref, o_ref):
    n = x_ref.shape[3]
    w1a = w1_ref[0, 0:CMP_STRIDE * NSA_DK, :]
    w1b = w1_ref[0, CMP_STRIDE * NSA_DK:, :]
    c = jnp.dot(pos_ref[...], w1_ref[0], preferred_element_type=F32)[0:1]
    outs = []
    for gg in range(2):
        xs = x_ref[0, 0, gg]
        y1 = jnp.dot(xs, w1a, preferred_element_type=F32)
        y2 = jnp.dot(xs, w1b, preferred_element_type=F32)
        hid = _gelu(y1 + pltpu.roll(y2, n - 1, 0) + c)
        outs.append(jnp.dot(hid.astype(BF16), w2_ref[0], preferred_element_type=F32))
    o_ref[0] = jnp.concatenate(outs, axis=1).astype(o_ref.dtype)


def _compress(z, pos, w1, w2):
    S = z.shape[0]
    n = S // CMP_STRIDE

    def blocks(c0):
        a = z[:, c0:c0 + 256].reshape(n, CMP_STRIDE, 2, 2, NSA_DK)
        return a.transpose(2, 3, 0, 1, 4).reshape(2, 2, n, CMP_STRIDE * NSA_DK)

    xs = jnp.stack([blocks(C_KCMP), blocks(C_VCMP)])
    pos8 = jnp.broadcast_to(pos.reshape(1, CMP_LEN * NSA_DK), (8, CMP_LEN * NSA_DK)).astype(BF16)
    return pl.pallas_call(
        _compress_kernel,
        grid=(2, 2),
        in_specs=[
            pl.BlockSpec((1, 1, 2, n, CMP_STRIDE * NSA_DK), lambda c, p: (c, p, 0, 0, 0)),
            pl.BlockSpec((8, CMP_LEN * NSA_DK), lambda c, p: (0, 0)),
            pl.BlockSpec((1, CMP_LEN * NSA_DK, CMP_HIDDEN), lambda c, p: (c, 0, 0)),
            pl.BlockSpec((1, CMP_HIDDEN, NSA_DK), lambda c, p: (c, 0, 0)),
        ],
        out_specs=pl.BlockSpec((1, n, LANES), lambda c, p: (c, 0, p)),
        out_shape=jax.ShapeDtypeStruct((2, n, 256), BF16),
        compiler_params=_params(("parallel", "parallel")),
        name="nsa_compress",
    )(xs, pos8, w1.astype(BF16), w2.astype(BF16))


def _stack_group_queries(q_ref, qs_ref, tq):
    lane = lax.broadcasted_iota(jnp.int32, (tq, LANES), 1)
    for h in range(NSA_HPG):
        q = q_ref[:, h * LANES:(h + 1) * LANES].astype(F32) * (NSA_DK ** -0.5)
        qs_ref[0, h * tq:(h + 1) * tq, :] = jnp.where(lane < NSA_DK, q, 0.0).astype(BF16)
        qs_ref[1, h * tq:(h + 1) * tq, :] = jnp.where(lane >= NSA_DK, q, 0.0).astype(BF16)


def _merge_pair_heads(o0, o1, o_ref, tq):
    lane = lax.broadcasted_iota(jnp.int32, (tq, LANES), 1)
    for h in range(NSA_HPG):
        o_ref[:, h * LANES:(h + 1) * LANES] = jnp.where(
            lane < NSA_DK, o0[h * tq:(h + 1) * tq], o1[h * tq:(h + 1) * tq]).astype(o_ref.dtype)


def _cmp_kernel(q_ref, kc_ref, vc_ref, pool_ref, o_ref, sel_ref, qs_ref, *, tq):
    qi = pl.program_id(1)
    n = kc_ref.shape[1]
    _stack_group_queries(q_ref, qs_ref, tq)
    t = qi * tq + lax.broadcasted_iota(jnp.int32, (tq, n), 0)
    cmp_end = lax.broadcasted_iota(jnp.int32, (tq, n), 1) * CMP_STRIDE + (CMP_LEN - 1)
    vis = (cmp_end <= t)[None]
    tb = qi * tq + lax.broadcasted_iota(jnp.int32, (tq, LANES), 0)
    blk = lax.broadcasted_iota(jnp.int32, (tq, LANES), 1)
    elig = blk * SEL_LEN <= tb
    cur = tb // SEL_LEN
    forced = (blk == 0) | (blk == cur) | (blk == cur - 1)
    outs, scores = [], []
    for gg in range(2):
        s = lax.dot_general(qs_ref[gg], kc_ref[0], (((1,), (1,)), ((), ())),
                            preferred_element_type=F32).reshape(NSA_HPG, tq, n)
        s = jnp.where(vis, s, NEG)
        m = jnp.max(s, axis=-1, keepdims=True)
        e = jnp.where(vis, jnp.exp(s - m), 0.0)
        den = jnp.sum(e, axis=-1, keepdims=True)
        p = e / jnp.where(den == 0.0, 1.0, den)
        outs.append(jnp.dot(p.reshape(NSA_HPG * tq, n).astype(BF16), vc_ref[0],
                            preferred_element_type=F32))
        imp = jnp.sum(p, axis=0)
        hi = imp.astype(BF16)
        r1 = imp - hi.astype(F32)
        mid = r1.astype(BF16)
        lo = (r1 - mid.astype(F32)).astype(BF16)
        pooled = (jnp.dot(hi, pool_ref[...], preferred_element_type=F32)
                  + jnp.dot(mid, pool_ref[...], preferred_element_type=F32)
                  + jnp.dot(lo, pool_ref[...], preferred_element_type=F32))
        scores.append(jnp.where(elig, pooled + jnp.where(forced, FORCE_BONUS, 0.0), -1.0))
    sels = [jnp.zeros((tq, LANES), F32), jnp.zeros((tq, LANES), F32)]
    for _ in range(SEL_TOPK):
        for gg in range(2):
            mx = jnp.max(scores[gg], axis=-1, keepdims=True)
            first = jnp.min(jnp.where(scores[gg] == mx, blk, LANES), axis=-1, keepdims=True)
            hit = blk == first
            sels[gg] = jnp.where(hit & elig, 1.0, sels[gg])
            scores[gg] = jnp.where(hit, -3e38, scores[gg])
    for gg in range(2):
        sel_ref[:, gg * LANES:(gg + 1) * LANES] = sels[gg].astype(sel_ref.dtype)
    _merge_pair_heads(outs[0], outs[1], o_ref, tq)


def _nsa_compressed(z, kvc, tq=256):
    S = z.shape[0]
    n = S // CMP_STRIDE
    pool = (np.arange(n)[:, None] // (SEL_LEN // CMP_STRIDE) == np.arange(LANES)[None, :])
    pool = jnp.asarray(pool, BF16)
    return pl.pallas_call(
        functools.partial(_cmp_kernel, tq=tq),
        grid=(2, S // tq),
        in_specs=[
            pl.BlockSpec((tq, 512), lambda p, qi: (qi, C_NQ // 512 + p)),
            pl.BlockSpec((1, n, LANES), lambda p, qi: (0, 0, p)),
            pl.BlockSpec((1, n, LANES), lambda p, qi: (1, 0, p)),
            pl.BlockSpec((n, LANES), lambda p, qi: (0, 0)),
        ],
        out_specs=[
            pl.BlockSpec((tq, 512), lambda p, qi: (qi, p)),
            pl.BlockSpec((tq, 256), lambda p, qi: (qi, p)),
        ],
        out_shape=[
            jax.ShapeDtypeStruct((S, NSA_COLS), BF16),
            jax.ShapeDtypeStruct((S, 512), BF16),
        ],
        scratch_shapes=[pltpu.VMEM((2, NSA_HPG * tq, LANES), BF16)],
        compiler_params=_params(("parallel", "parallel")),
        name="nsa_cmp_topk",
    )(z, kvc, kvc, pool)


def _sel_kernel(q_ref, k_ref, v_ref, sel_ref, ex_ref, o_ref, qs_ref, m_ref, l_ref, acc_ref,
                *, tq, tk):
    qi = pl.program_id(1)
    _stack_group_queries(q_ref, qs_ref, tq)
    m_ref[...] = jnp.full(m_ref.shape, NEG, F32)
    l_ref[...] = jnp.zeros(l_ref.shape, F32)
    acc_ref[...] = jnp.zeros(acc_ref.shape, F32)
    row = qi * tq + lax.broadcasted_iota(jnp.int32, (tq, tk), 0)
    col = lax.broadcasted_iota(jnp.int32, (tq, tk), 1)

    def body(ki, carry):
        off = pl.multiple_of(ki * tk, tk)
        k = k_ref[pl.ds(off, tk), :]
        v = v_ref[pl.ds(off, tk), :]
        causal = off + col <= row
        for gg in range(2):
            hit = jnp.dot(sel_ref[:, gg * LANES:(gg + 1) * LANES], ex_ref[ki],
                          preferred_element_type=F32)
            bias = jnp.where(causal & (hit > 0.5), 0.0, NEG)
            s = lax.dot_general(qs_ref[gg], k, (((1,), (1,)), ((), ())),
                                preferred_element_type=F32)
            s = (s.reshape(NSA_HPG, tq, tk) + bias[None]).reshape(NSA_HPG * tq, tk)
            _online_softmax_step(s, v, m_ref.at[gg], l_ref.at[gg], acc_ref.at[gg])
        return carry

    lax.fori_loop(0, (qi * tq + tq - 1) // tk + 1, body, 0)
    _merge_pair_heads(acc_ref[0] / l_ref[0], acc_ref[1] / l_ref[1], o_ref, tq)


def _nsa_selected(z, sel, tq=256, tk=512):
    S = z.shape[0]
    tk = min(tk, S)
    expand = np.arange(LANES)[:, None] == np.arange(S)[None, :] // SEL_LEN
    expand = jnp.asarray(expand.reshape(LANES, S // tk, tk).transpose(1, 0, 2), BF16)
    return pl.pallas_call(
        functools.partial(_sel_kernel, tq=tq, tk=tk),
        grid=(2, S // tq),
        in_specs=[
            pl.BlockSpec((tq, 512), lambda p, qi: (qi, C_NQ // 512 + p)),
            pl.BlockSpec((S, LANES), lambda p, qi: (0, C_KSEL // LANES + p)),
            pl.BlockSpec((S, LANES), lambda p, qi: (0, C_VSEL // LANES + p)),
            pl.BlockSpec((tq, 256), lambda p, qi: (qi, p)),
            pl.BlockSpec((S // tk, LANES, tk), lambda p, qi: (0, 0, 0)),
        ],
        out_specs=pl.BlockSpec((tq, 512), lambda p, qi: (qi, p)),
        out_shape=jax.ShapeDtypeStruct((S, NSA_COLS), BF16),
        scratch_shapes=[
            pltpu.VMEM((2, NSA_HPG * tq, LANES), BF16),
            pltpu.VMEM((2, NSA_HPG * tq, LANES), F32),
            pltpu.VMEM((2, NSA_HPG * tq, LANES), F32),
            pltpu.VMEM((2, NSA_HPG * tq, LANES), F32),
        ],
        compiler_params=_params(("parallel", "arbitrary")),
        name="nsa_selected",
    )(z, z, z, sel, expand)


def _win_kernel(q_ref, *refs, tq, nprev):
    k_refs = refs[:nprev + 1]
    v_refs = refs[nprev + 1:2 * nprev + 2]
    o_ref, qs_ref = refs[2 * nprev + 2:]
    qi = pl.program_id(1)
    nk = (nprev + 1) * tq
    _stack_group_queries(q_ref, qs_ref, tq)
    k = jnp.concatenate([r[...] for r in k_refs], axis=0)
    v = jnp.concatenate([r[...] for r in v_refs], axis=0)
    t = qi * tq + lax.broadcasted_iota(jnp.int32, (tq, nk), 0)
    kpos = (qi - nprev) * tq + lax.broadcasted_iota(jnp.int32, (tq, nk), 1)
    dist = t - kpos
    bias = jnp.where((dist >= 0) & (dist < WINDOW) & (kpos >= 0), 0.0, NEG)[None]
    outs = []
    for gg in range(2):
        s = lax.dot_general(qs_ref[gg], k, (((1,), (1,)), ((), ())), preferred_element_type=F32)
        s = s.reshape(NSA_HPG, tq, nk) + bias
        m = jnp.max(s, axis=-1, keepdims=True)
        e = jnp.exp(s - m)
        p = e / jnp.sum(e, axis=-1, keepdims=True)
        outs.append(jnp.dot(p.reshape(NSA_HPG * tq, nk).astype(BF16), v,
                            preferred_element_type=F32))
    _merge_pair_heads(outs[0], outs[1], o_ref, tq)


def _nsa_window(z, tq=256):
    S = z.shape[0]
    nprev = WINDOW // tq

    def kv_spec(off, back):
        return pl.BlockSpec((tq, LANES), lambda p, qi: (jnp.maximum(qi - back, 0), off + p))

    backs = list(range(nprev, -1, -1))
    return pl.pallas_call(
        functools.partial(_win_kernel, tq=tq, nprev=nprev),
        grid=(2, S // tq),
        in_specs=[pl.BlockSpec((tq, 512), lambda p, qi: (qi, C_NQ // 512 + p))]
        + [kv_spec(C_KWIN // LANES, b) for b in backs]
        + [kv_spec(C_VWIN // LANES, b) for b in backs],
        out_specs=pl.BlockSpec((tq, 512), lambda p, qi: (qi, p)),
        out_shape=jax.ShapeDtypeStruct((S, NSA_COLS), BF16),
        scratch_shapes=[pltpu.VMEM((2, NSA_HPG * tq, LANES), BF16)],
        compiler_params=_params(("parallel", "parallel")),
        name="nsa_window",
    )(z, *([z] * (2 * nprev + 2)))


def _memkv_kernel(mem_ref, g_ref, w_ref, o_ref):
    memn = _rms(mem_ref[...], g_ref[...]).astype(BF16)
    o_ref[...] = jnp.dot(memn, w_ref[...], preferred_element_type=F32).astype(o_ref.dtype)


def _memkv(mem, g, w):
    M = mem.shape[0]
    return pl.pallas_call(
        _memkv_kernel,
        out_shape=jax.ShapeDtypeStruct((M, 2 * X_HEADS * X_DH), BF16),
        compiler_params=pltpu.CompilerParams(vmem_limit_bytes=VMEM_LIMIT),
        name="mem_kv",
    )(mem, g, w)


def _mid_kernel(x_ref, oda_ref, oc_ref, os_ref, ow_ref, ng_ref, bg0_ref, bg1_ref, eg_ref,
                wda_ref, wnsa_ref, wout_ref, gx_ref, wxq_ref, mkv_ref, wxo_ref, gf_ref,
                h_ref, xnt_ref):
    ng = ng_ref[...]
    o_nsa = jnp.zeros(oc_ref.shape, F32)
    for c, br in enumerate((oc_ref, os_ref, ow_ref)):
        gate = jax.nn.sigmoid(jnp.dot(ng, eg_ref[c], preferred_element_type=F32))
        o_nsa = o_nsa + gate * br[...].astype(F32)
    y_da = jnp.dot(oda_ref[...], wda_ref[...], preferred_element_type=F32)
    y_nsa = jnp.dot(o_nsa.astype(BF16), wnsa_ref[...], preferred_element_type=F32)
    mix = (jax.nn.sigmoid(bg0_ref[...].astype(F32)) * y_da
           + jax.nn.sigmoid(bg1_ref[...].astype(F32)) * y_nsa)
    h = x_ref[...] + jnp.dot(mix.astype(BF16), wout_ref[...], preferred_element_type=F32)
    hn = _rms(h, gx_ref[...]).astype(BF16)
    q = (jnp.dot(hn, wxq_ref[...], preferred_element_type=F32) * (X_DH ** -0.5)).astype(BF16)
    outs = []
    for hd in range(X_HEADS):
        k = mkv_ref[:, hd * X_DH:(hd + 1) * X_DH]
        v = mkv_ref[:, (X_HEADS + hd) * X_DH:(X_HEADS + hd + 1) * X_DH]
        s = lax.dot_general(q[:, hd * X_DH:(hd + 1) * X_DH], k, (((1,), (1,)), ((), ())),
                            preferred_element_type=F32)
        e = jnp.exp(s - jnp.max(s, axis=-1, keepdims=True))
        p = (e / jnp.sum(e, axis=-1, keepdims=True)).astype(BF16)
        outs.append(jnp.dot(p, v, preferred_element_type=F32))
    o = jnp.concatenate(outs, axis=1).astype(BF16)
    h = h + jnp.dot(o, wxo_ref[...], preferred_element_type=F32)
    h_ref[...] = h
    xnt_ref[...] = _rms(h, gf_ref[...]).T.astype(BF16)


def _mid(x, z, o_da, o_c, o_s, o_w, eg, wda, wnsa, wout, gx, wxq, mkv, wxo, gf, tm=256):
    S = x.shape[0]
    M = mkv.shape[0]
    row = lambda w: pl.BlockSpec((tm, w), lambda i: (i, 0))
    return pl.pallas_call(
        _mid_kernel,
        grid=(S // tm,),
        in_specs=[
            row(D_MODEL), row(1024), row(NSA_COLS), row(NSA_COLS), row(NSA_COLS),
            pl.BlockSpec((tm, LANES), lambda i: (i, C_NGB // LANES)),
            pl.BlockSpec((tm, D_MODEL), lambda i: (i, C_BG // D_MODEL)),
            pl.BlockSpec((tm, D_MODEL), lambda i: (i, C_BG // D_MODEL + 1)),
            _const_spec((3, LANES, NSA_COLS)),
            _const_spec((1024, D_MODEL)), _const_spec((NSA_COLS, D_MODEL)),
            _const_spec((D_MODEL, D_MODEL)),
            _const_spec((1, D_MODEL)), _const_spec((D_MODEL, X_HEADS * X_DH)),
            _const_spec((M, 2 * X_HEADS * X_DH)), _const_spec((X_HEADS * X_DH, D_MODEL)),
            _const_spec((1, D_MODEL)),
        ],
        out_specs=[
            pl.BlockSpec((tm, D_MODEL), lambda i: (i, 0)),
            pl.BlockSpec((D_MODEL, tm), lambda i: (0, i)),
        ],
        out_shape=[
            jax.ShapeDtypeStruct((S, D_MODEL), F32),
            jax.ShapeDtypeStruct((D_MODEL, S), BF16),
        ],
        compiler_params=_params(("parallel",)),
        name="merge_xattn",
    )(x, o_da, o_c, o_s, o_w, z, z, z, eg, wda, wnsa, wout, gx, wxq, mkv, wxo, gf)


def _gate_expand():
    e = np.zeros((3, LANES, NSA_COLS), np.float32)
    for c in range(3):
        for g in range(NSA_GROUPS):
            for h in range(NSA_HPG):
                col = ((g // 2) * NSA_HPG + h) * LANES + (g % 2) * NSA_DK
                e[c, NG_LANE0 + g * 12 + h * 3 + c, col:col + NSA_DK] = 1.0
    return jnp.asarray(e, BF16)


def _topk_rows(x, k):
    vals = []
    for _ in range(k):
        m = jnp.max(x, axis=0, keepdims=True)
        vals.append(m)
        x = jnp.where(x == m, -3e38, x)
    return vals


def _peer_pre_kernel(xnt_ref, wqt_ref, k1_ref, k2_ref, t1_ref, s2_ref, e1_ref, e2_ref):
    half = PEER_DQ // 2
    qt = jnp.dot(wqt_ref[...], xnt_ref[...], preferred_element_type=F32).astype(BF16)
    for h in range(PEER_HEADS):
        s1 = jnp.dot(k1_ref[...], qt[h * PEER_DQ:h * PEER_DQ + half], preferred_element_type=F32)
        s2 = jnp.dot(k2_ref[...], qt[h * PEER_DQ + half:(h + 1) * PEER_DQ],
                     preferred_element_type=F32)
        v1 = _topk_rows(s1, PEER_TOPK + 1)
        v2r = _topk_rows(s2, PEER_TOPK + 1)
        v2 = jnp.concatenate(v2r[:PEER_TOPK], axis=0)
        cand = jnp.concatenate([a + v2 for a in v1[:PEER_TOPK]], axis=0)
        top = _topk_rows(cand, PEER_TOPK + 1)
        nxt = jnp.maximum(top[PEER_TOPK], jnp.maximum(v1[PEER_TOPK] + v2r[0], v1[0] + v2r[PEER_TOPK]))
        thr = 0.5 * (top[PEER_TOPK - 1] + nxt)
        zsum = jnp.sum(jnp.where(cand > thr, jnp.exp(cand - top[0]), 0.0), axis=0, keepdims=True)
        t1_ref[h] = thr - s1
        s2_ref[h] = s2
        e1_ref[h] = jnp.exp(s1 - v1[0])
        e2_ref[h] = jnp.exp(s2 - v2[0:1]) / zsum


def _peer_pre(xnt, wqt, k1, k2, tt=256):
    S = xnt.shape[1]
    big = lambda: pl.BlockSpec((PEER_HEADS, PEER_NKEYS, tt), lambda i: (0, 0, i))
    sds = jax.ShapeDtypeStruct((PEER_HEADS, PEER_NKEYS, S), F32)
    return pl.pallas_call(
        _peer_pre_kernel,
        grid=(S // tt,),
        in_specs=[
            pl.BlockSpec((D_MODEL, tt), lambda i: (0, i)),
            _const_spec((PEER_HEADS * PEER_DQ, D_MODEL)),
            _const_spec((PEER_NKEYS, PEER_DQ // 2)),
            _const_spec((PEER_NKEYS, PEER_DQ // 2)),
        ],
        out_specs=[big(), big(), big(), big()],
        out_shape=[sds, sds, sds, sds],
        compiler_params=_params(("parallel",)),
        name="peer_scores",
    )(xnt, wqt, k1, k2)


PEER_JB = 32


def _peer_kernel(xnt_ref, u_ref, vt_ref, t1_ref, s2_ref, e1_ref, e2_ref, o_ref, a_ref, wg_ref,
                 wgp_ref, *, ic):
    e = pl.program_id(1)
    ne = pl.num_programs(1) - 1
    tt = xnt_ref.shape[1]

    @pl.when(e == 0)
    def _():
        o_ref[...] = jnp.zeros(o_ref.shape, F32)
        wg_ref[...] = jnp.zeros(wg_ref.shape, BF16)

    def apply_previous():
        wgp_ref[...] = wg_ref[...]
        o_ref[...] += jnp.dot(vt_ref[...], wgp_ref[...], preferred_element_type=F32)

    @pl.when(e < ne)
    def _():
        a_ref[...] = jnp.dot(u_ref[...], xnt_ref[...], preferred_element_type=F32)
        apply_previous()
        nblk = PEER_NKEYS // PEER_JB
        for ii in range(ic):
            for c in range(tt // LANES):
                cols = slice(c * LANES, (c + 1) * LANES)
                taus = [t1_ref[h, 0, ii:ii + 1, cols] for h in range(PEER_HEADS)]
                e1s = [e1_ref[h, 0, ii:ii + 1, cols] for h in range(PEER_HEADS)]
                for jb in range(nblk):
                    j = slice(jb * PEER_JB, (jb + 1) * PEER_JB)
                    rows = slice(ii * PEER_NKEYS + jb * PEER_JB,
                                 ii * PEER_NKEYS + (jb + 1) * PEER_JB)
                    w = jnp.zeros((PEER_JB, LANES), F32)
                    for h in range(PEER_HEADS):
                        w = w + jnp.where(s2_ref[h, j, cols] > taus[h],
                                          e2_ref[h, j, cols] * e1s[h], 0.0)
                    wg_ref[rows, cols] = (_gelu(a_ref[rows, cols]) * w).astype(BF16)

    @pl.when(e == ne)
    def _():
        apply_previous()


def _peer(xnt, u, vt, t1, s2, e1, e2, tt=512, ic=4):
    S = xnt.shape[1]
    ec = ic * PEER_NKEYS
    ne = (PEER_NKEYS * PEER_NKEYS) // ec
    tok = pl.BlockSpec((PEER_HEADS, PEER_NKEYS, tt), lambda i, e: (0, 0, i))
    chunk = pl.BlockSpec((PEER_HEADS, 1, ic, tt), lambda i, e: (0, jnp.minimum(e, ne - 1), 0, i))
    t1 = t1.reshape(PEER_HEADS, ne, ic, S)
    e1 = e1.reshape(PEER_HEADS, ne, ic, S)
    return pl.pallas_call(
        functools.partial(_peer_kernel, ic=ic),
        grid=(S // tt, ne + 1),
        in_specs=[
            pl.BlockSpec((D_MODEL, tt), lambda i, e: (0, i)),
            pl.BlockSpec((ec, D_MODEL), lambda i, e: (jnp.minimum(e, ne - 1), 0)),
            pl.BlockSpec((D_MODEL, ec), lambda i, e: (0, jnp.maximum(e - 1, 0))),
            chunk, tok, chunk, tok,
        ],
        out_specs=pl.BlockSpec((D_MODEL, tt), lambda i, e: (0, i)),
        out_shape=jax.ShapeDtypeStruct((D_MODEL, S), F32),
        scratch_shapes=[pltpu.VMEM((ec, tt), F32), pltpu.VMEM((ec, tt), BF16),
                        pltpu.VMEM((ec, tt), BF16)],
        compiler_params=_params(("parallel", "arbitrary")),
        name="peer_experts",
    )(xnt, u, vt, t1, s2, e1, e2)


def _final_kernel(h_ref, yt_ref, g_ref, o_ref):
    o_ref[...] = _rms(h_ref[...] + yt_ref[...].T, g_ref[...])


def _final(h, yt, g, tm=256):
    S = h.shape[0]
    return pl.pallas_call(
        _final_kernel,
        grid=(S // tm,),
        in_specs=[
            pl.BlockSpec((tm, D_MODEL), lambda i: (i, 0)),
            pl.BlockSpec((D_MODEL, tm), lambda i: (0, i)),
            pl.BlockSpec((1, D_MODEL), lambda i: (0, 0)),
        ],
        out_specs=pl.BlockSpec((tm, D_MODEL), lambda i: (i, 0)),
        out_shape=jax.ShapeDtypeStruct((S, D_MODEL), F32),
        compiler_params=_params(("parallel",)),
        name="final_norm",
    )(h, yt, g)


def kernel(x, mem, positions, norm_mix_g, w_in, da_lambda_q1, da_lambda_k1, da_lambda_q2, da_lambda_k2, da_subln_g, nsa_cmp_pos, nsa_cmpk_w1, nsa_cmpk_w2, nsa_cmpv_w1, nsa_cmpv_w2, w_branch_da, w_branch_nsa, w_out, norm_x_g, norm_mem_g, w_xq, w_xkv, w_xo, norm_ffn_g, peer_wq, peer_keys1, peer_keys2, peer_u, peer_v, norm_final_g):
    B, S, D = x.shape
    assert B == 1 and D == D_MODEL and norm_mix_g.shape[0] == 1
    l = 0
    lambda_init = 0.8 - 0.6 * math.exp(-0.3 * l)
    lam = (jnp.exp(jnp.sum((da_lambda_q1[l] * da_lambda_k1[l]).astype(F32)))
           - jnp.exp(jnp.sum((da_lambda_q2[l] * da_lambda_k2[l]).astype(F32))) + lambda_init)
    lam_row = jnp.full((1, LANES), lam, F32)

    z = _inproj(x[0], norm_mix_g[l][None], _pack_w_in(w_in[l]), *_rope_tables(positions[0]))
    o_da = _diff_attention(z, lam_row, da_subln_g[l][None], lambda_init)
    kvc = _compress(z, nsa_cmp_pos[l], jnp.stack([nsa_cmpk_w1[l], nsa_cmpv_w1[l]]),
                    jnp.stack([nsa_cmpk_w2[l], nsa_cmpv_w2[l]]))
    o_c, sel = _nsa_compressed(z, kvc)
    o_s = _nsa_selected(z, sel)
    o_w = _nsa_window(z)

    wnsa = w_branch_nsa[l].reshape(2, 2, NSA_HPG, NSA_DK, D).transpose(0, 2, 1, 3, 4)
    wnsa = wnsa.reshape(NSA_COLS, D).astype(BF16)
    mkv = _memkv(mem[0], norm_mem_g[l][None], w_xkv[l].astype(BF16))
    h, xnt = _mid(x[0], z, o_da, o_c, o_s, o_w, _gate_expand(), w_branch_da[l].astype(BF16), wnsa,
                  w_out[l].astype(BF16), norm_x_g[l][None], w_xq[l].astype(BF16), mkv,
                  w_xo[l].astype(BF16), norm_ffn_g[l][None])

    t1, s2, e1, e2 = _peer_pre(xnt, peer_wq[l].T.astype(BF16), peer_keys1[l].astype(BF16),
                               peer_keys2[l].astype(BF16))
    yt = _peer(xnt, peer_u[l].astype(BF16), peer_v[l].T.astype(BF16), t1, s2, e1, e2)
    return _final(h, yt, norm_final_g[None])[None]
```

```python
import functools
import math

import jax
import jax.numpy as jnp
import numpy as np
from jax import lax
from jax.experimental import pallas as pl
from jax.experimental.pallas import tpu as pltpu

F32 = jnp.float32
BF16 = jnp.bfloat16

D_MODEL = 2048
EPS = 1e-6
NEG = -1e30
ROPE_THETA = 500000.0
DA_HEADS = 8
DA_DK = 64
DA_DV = 128
NSA_GROUPS = 4
NSA_HPG = 4
NSA_DK = 64
CMP_LEN = 32
CMP_STRIDE = 16
CMP_HIDDEN = 128
SEL_LEN = 64
SEL_TOPK = 16
WINDOW = 512
FORCE_BONUS = 100.0
ROPE_DIM = 16
X_HEADS = 4
X_DH = 128
PEER_HEADS = 8
PEER_NKEYS = 128
PEER_DQ = 256
PEER_TOPK = 16

LANES = 128
VMEM_LIMIT = 56 * 1024 * 1024

C_ZQ = 0
C_ZK = 1024
C_NQ = 2048
C_KCMP = 3072
C_KSEL = 3328
C_KWIN = 3584
C_NGB = 3840
NG_LANE0 = 16
N_ROPE_COLS = 4096
C_ZV = 4096
C_VCMP = 5120
C_VSEL = 5376
C_VWIN = 5632
C_BG = 6144
N_COLS = 10240
TN_IN = 1024
NSA_COLS = 1024


def _params(sem):
    return pltpu.CompilerParams(dimension_semantics=sem, vmem_limit_bytes=VMEM_LIMIT)


def _const_spec(shape):
    n = len(shape)
    return pl.BlockSpec(shape, lambda *_: (0,) * n, pipeline_mode=pl.Buffered(1))


def _gelu(x):
    return 0.5 * x * (1.0 + lax.erf(x * (2.0 ** -0.5)))


def _rms(x, g):
    ms = jnp.mean(x * x, axis=-1, keepdims=True)
    return x * lax.rsqrt(ms + EPS) * g


def _inproj_kernel(x_ref, g_ref, w_ref, c_ref, sp_ref, sm_ref, o_ref, xn_ref):
    j = pl.program_id(1)

    @pl.when(j == 0)
    def _():
        xn_ref[...] = _rms(x_ref[...], g_ref[...]).astype(BF16)

    z = jnp.dot(xn_ref[...], w_ref[...], preferred_element_type=F32)

    @pl.when(j < N_ROPE_COLS // TN_IN)
    def _():
        c = c_ref[...]
        sp = sp_ref[...]
        sm = sm_ref[...]
        for k in range(TN_IN // LANES):
            zc = z[:, k * LANES:(k + 1) * LANES]
            r = zc * c + pltpu.roll(zc, 8, 1) * sp + pltpu.roll(zc, LANES - 8, 1) * sm
            o_ref[:, k * LANES:(k + 1) * LANES] = r.astype(o_ref.dtype)

    @pl.when(j >= N_ROPE_COLS // TN_IN)
    def _():
        o_ref[...] = z.astype(o_ref.dtype)


def _inproj(x, g, w, rope_c, rope_sp, rope_sm, tm=512):
    S = x.shape[0]
    return pl.pallas_call(
        _inproj_kernel,
        grid=(S // tm, N_COLS // TN_IN),
        in_specs=[
            pl.BlockSpec((tm, D_MODEL), lambda i, j: (i, 0)),
            pl.BlockSpec((1, D_MODEL), lambda i, j: (0, 0)),
            pl.BlockSpec((D_MODEL, TN_IN), lambda i, j: (0, j)),
            pl.BlockSpec((tm, LANES), lambda i, j: (i, 0)),
            pl.BlockSpec((tm, LANES), lambda i, j: (i, 0)),
            pl.BlockSpec((tm, LANES), lambda i, j: (i, 0)),
        ],
        out_specs=pl.BlockSpec((tm, TN_IN), lambda i, j: (i, j)),
        out_shape=jax.ShapeDtypeStruct((S, N_COLS), BF16),
        scratch_shapes=[pltpu.VMEM((tm, D_MODEL), BF16)],
        compiler_params=_params(("parallel", "arbitrary")),
        name="inproj",
    )(x, g, w, rope_c, rope_sp, rope_sm)


def _pack_w_in(w_in):
    zq, zk, zv, nq, nkv, ng, bg = jnp.split(
        w_in.astype(BF16), np.cumsum([1024, 1024, 1024, 1024, 1536, 48]).tolist(), axis=1)
    D = w_in.shape[0]
    nq = nq.reshape(D, 2, 2, NSA_HPG, NSA_DK).transpose(0, 1, 3, 2, 4).reshape(D, 1024)
    nkv = nkv.reshape(D, 6, 256)
    ngb = jnp.concatenate([jnp.zeros((D, NG_LANE0), BF16), ng,
                           jnp.zeros((D, 256 - NG_LANE0 - 48), BF16)], axis=1)
    pad = jnp.zeros((D, C_BG - C_VWIN - 256), BF16)
    return jnp.concatenate([zq, zk, nq, nkv[:, 0], nkv[:, 2], nkv[:, 4], ngb,
                            zv, nkv[:, 1], nkv[:, 3], nkv[:, 5], pad, bg], axis=1)


def _rope_tables(positions):
    inv = ROPE_THETA ** (-jnp.arange(0, ROPE_DIM, 2, dtype=F32) / ROPE_DIM)
    ang = positions.astype(F32)[:, None] * inv
    cos, sin = jnp.cos(ang), jnp.sin(ang)
    S = positions.shape[0]
    one = jnp.ones((S, 48), F32)
    zero = jnp.zeros((S, 48), F32)
    z8 = jnp.zeros((S, 8), F32)
    c64 = jnp.concatenate([cos, cos, one], axis=1)
    sp64 = jnp.concatenate([z8, sin, zero], axis=1)
    sm64 = jnp.concatenate([-sin, z8, zero], axis=1)
    return (jnp.tile(c64, (1, 2)), jnp.tile(sp64, (1, 2)), jnp.tile(sm64, (1, 2)))


def _online_softmax_step(s, v, m_ref, l_ref, acc_ref):
    m_prev = m_ref[...]
    m_new = jnp.maximum(m_prev, jnp.max(s, axis=-1, keepdims=True))
    alpha = jnp.exp(m_prev - m_new)
    p = jnp.exp(s - m_new[:, 0:1])
    l_ref[...] = alpha * l_ref[...] + jnp.sum(p, axis=-1, keepdims=True)
    acc_ref[...] = alpha * acc_ref[...] + jnp.dot(p.astype(BF16), v, preferred_element_type=F32)
    m_ref[...] = m_new


DA_SPLIT = 2
DA_WIDE = 4


def _da_kernel(lam_ref, q_ref, k_ref, v_ref, g_ref, o_ref, qs_ref, m_ref, l_ref, acc_ref,
               *, tq, out_scale):
    qi = pl.program_id(1)
    q = q_ref[...].astype(F32) * (DA_DK ** -0.5)
    lane = lax.broadcasted_iota(jnp.int32, q.shape, 1)
    qs_ref[0:tq, :] = jnp.where(lane < DA_DK, q, 0.0).astype(BF16)
    qs_ref[tq:2 * tq, :] = jnp.where(lane >= DA_DK, q, 0.0).astype(BF16)
    m_ref[...] = jnp.full(m_ref.shape, NEG, F32)
    l_ref[...] = jnp.zeros(l_ref.shape, F32)
    acc_ref[...] = jnp.zeros(acc_ref.shape, F32)

    rc = tq // DA_SPLIT

    def tile(off, width, masked):
        off = pl.multiple_of(off, tq)
        k = k_ref[pl.ds(off, width), :]
        v = v_ref[pl.ds(off, width), :]
        for c in range(2 * DA_SPLIT):
            r = slice(c * rc, (c + 1) * rc)
            s = lax.dot_general(qs_ref[r, :], k, (((1,), (1,)), ((), ())),
                                preferred_element_type=F32)
            if masked:
                row = (c % DA_SPLIT) * rc + lax.broadcasted_iota(jnp.int32, (rc, width), 0)
                col = lax.broadcasted_iota(jnp.int32, (rc, width), 1)
                s = jnp.where(col <= row, s, NEG)
            _online_softmax_step(s, v, m_ref.at[r], l_ref.at[r], acc_ref.at[r])

    def wide(kw, carry):
        tile(kw * (DA_WIDE * tq), DA_WIDE * tq, False)
        return carry

    def narrow(ki, carry):
        tile(ki * tq, tq, False)
        return carry

    nw = qi // DA_WIDE
    lax.fori_loop(0, nw, wide, 0)
    rem = qi - nw * DA_WIDE

    @pl.when(rem >= 2)
    def _():
        tile(nw * (DA_WIDE * tq), 2 * tq, False)

    lax.fori_loop(nw * DA_WIDE + 2 * (rem // 2), qi, narrow, 0)
    tile(qi * tq, tq, True)

    o = acc_ref[...] / l_ref[...]
    d = o[0:tq] - lam_ref[...] * o[tq:2 * tq]
    o_ref[...] = (_rms(d, g_ref[...]) * out_scale).astype(o_ref.dtype)


def _diff_attention(z, lam_row, subln_g, lambda_init, tq=512):
    S = z.shape[0]
    return pl.pallas_call(
        functools.partial(_da_kernel, tq=tq, out_scale=1.0 - lambda_init),
        grid=(DA_HEADS, S // tq),
        in_specs=[
            pl.BlockSpec((1, LANES), lambda h, qi: (0, 0)),
            pl.BlockSpec((tq, LANES), lambda h, qi: (qi, C_ZQ // LANES + h)),
            pl.BlockSpec((S, LANES), lambda h, qi: (0, C_ZK // LANES + h)),
            pl.BlockSpec((S, LANES), lambda h, qi: (0, C_ZV // LANES + h)),
            pl.BlockSpec((1, LANES), lambda h, qi: (0, 0)),
        ],
        out_specs=pl.BlockSpec((tq, LANES), lambda h, qi: (qi, h)),
        out_shape=jax.ShapeDtypeStruct((S, DA_HEADS * DA_DV), BF16),
        scratch_shapes=[
            pltpu.VMEM((2 * tq, LANES), BF16),
            pltpu.VMEM((2 * tq, LANES), F32),
            pltpu.VMEM((2 * tq, LANES), F32),
            pltpu.VMEM((2 * tq, LANES), F32),
        ],
        compiler_params=_params(("parallel", "arbitrary")),
        name="diff_attn",
    )(lam_row, z, z, z, subln_g)


def _compress_kernel(x_ref, pos_ref, w1_ref, w2_ref, o_ref):
    n = x_ref.shape[3]
    w1a = w1_ref[0, 0:CMP_STRIDE * NSA_DK, :]
    w1b = w1_ref[0, CMP_STRIDE * NSA_DK:, :]
    c = jnp.dot(pos_ref[...], w1_ref[0], preferred_element_type=F32)[0:1]
    outs = []
    for gg in range(2):
        xs = x_ref[0, 0, gg]
        y1 = jnp.dot(xs, w1a, preferred_element_type=F32)
        y2 = jnp.dot(xs, w1b, preferred_element_type=F32)
        hid = _gelu(y1 + pltpu.roll(y2, n - 1, 0) + c)
        outs.append(jnp.dot(hid.astype(BF16), w2_ref[0], preferred_element_type=F32))
    o_ref[0] = jnp.concatenate(outs, axis=1).astype(o_ref.dtype)


def _compress(z, pos, w1, w2):
    S = z.shape[0]
    n = S // CMP_STRIDE

    def blocks(c0):
        a = z[:, c0:c0 + 256].reshape(n, CMP_STRIDE, 2, 2, NSA_DK)
        return a.transpose(2, 3, 0, 1, 4).reshape(2, 2, n, CMP_STRIDE * NSA_DK)

    xs = jnp.stack([blocks(C_KCMP), blocks(C_VCMP)])
    pos8 = jnp.broadcast_to(pos.reshape(1, CMP_LEN * NSA_DK), (8, CMP_LEN * NSA_DK)).astype(BF16)
    return pl.pallas_call(
        _compress_kernel,
        grid=(2, 2),
        in_specs=[
            pl.BlockSpec((1, 1, 2, n, CMP_STRIDE * NSA_DK), lambda c, p: (c, p, 0, 0, 0)),
            pl.BlockSpec((8, CMP_LEN * NSA_DK), lambda c, p: (0, 0)),
            pl.BlockSpec((1, CMP_LEN * NSA_DK, CMP_HIDDEN), lambda c, p: (c, 0, 0)),
            pl.BlockSpec((1, CMP_HIDDEN, NSA_DK), lambda c, p: (c, 0, 0)),
        ],
        out_specs=pl.BlockSpec((1, n, LANES), lambda c, p: (c, 0, p)),
        out_shape=jax.ShapeDtypeStruct((2, n, 256), BF16),
        compiler_params=_params(("parallel", "parallel")),
        name="nsa_compress",
    )(xs, pos8, w1.astype(BF16), w2.astype(BF16))


def _stack_group_queries(q_ref, qs_ref, tq):
    lane = lax.broadcasted_iota(jnp.int32, (tq, LANES), 1)
    for h in range(NSA_HPG):
        q = q_ref[:, h * LANES:(h + 1) * LANES].astype(F32) * (NSA_DK ** -0.5)
        qs_ref[0, h * tq:(h + 1) * tq, :] = jnp.where(lane < NSA_DK, q, 0.0).astype(BF16)
        qs_ref[1, h * tq:(h + 1) * tq, :] = jnp.where(lane >= NSA_DK, q, 0.0).astype(BF16)


def _merge_pair_heads(o0, o1, o_ref, tq):
    lane = lax.broadcasted_iota(jnp.int32, (tq, LANES), 1)
    for h in range(NSA_HPG):
        o_ref[:, h * LANES:(h + 1) * LANES] = jnp.where(
            lane < NSA_DK, o0[h * tq:(h + 1) * tq], o1[h * tq:(h + 1) * tq]).astype(o_ref.dtype)


def _cmp_kernel(q_ref, kc_ref, vc_ref, pool_ref, o_ref, sel_ref, qs_ref, *, tq):
    qi = pl.program_id(1)
    n = kc_ref.shape[1]
    _stack_group_queries(q_ref, qs_ref, tq)
    t = qi * tq + lax.broadcasted_iota(jnp.int32, (tq, n), 0)
    cmp_end = lax.broadcasted_iota(jnp.int32, (tq, n), 1) * CMP_STRIDE + (CMP_LEN - 1)
    vis = (cmp_end <= t)[None]
    tb = qi * tq + lax.broadcasted_iota(jnp.int32, (tq, LANES), 0)
    blk = lax.broadcasted_iota(jnp.int32, (tq, LANES), 1)
    elig = blk * SEL_LEN <= tb
    cur = tb // SEL_LEN
    forced = (blk == 0) | (blk == cur) | (blk == cur - 1)
    outs, scores = [], []
    for gg in range(2):
        s = lax.dot_general(qs_ref[gg], kc_ref[0], (((1,), (1,)), ((), ())),
                            preferred_element_type=F32).reshape(NSA_HPG, tq, n)
        s = jnp.where(vis, s, NEG)
        m = jnp.max(s, axis=-1, keepdims=True)
        e = jnp.where(vis, jnp.exp(s - m), 0.0)
        den = jnp.sum(e, axis=-1, keepdims=True)
        p = e / jnp.where(den == 0.0, 1.0, den)
        outs.append(jnp.dot(p.reshape(NSA_HPG * tq, n).astype(BF16), vc_ref[0],
                            preferred_element_type=F32))
        imp = jnp.sum(p, axis=0)
        hi = imp.astype(BF16)
        r1 = imp - hi.astype(F32)
        mid = r1.astype(BF16)
        lo = (r1 - mid.astype(F32)).astype(BF16)
        pooled = (jnp.dot(hi, pool_ref[...], preferred_element_type=F32)
                  + jnp.dot(mid, pool_ref[...], preferred_element_type=F32)
                  + jnp.dot(lo, pool_ref[...], preferred_element_type=F32))
        scores.append(jnp.where(elig, pooled + jnp.where(forced, FORCE_BONUS, 0.0), -1.0))
    sels = [jnp.zeros((tq, LANES), F32), jnp.zeros((tq, LANES), F32)]
    for _ in range(SEL_TOPK):
        for gg in range(2):
            mx = jnp.max(scores[gg], axis=-1, keepdims=True)
            first = jnp.min(jnp.where(scores[gg] == mx, blk, LANES), axis=-1, keepdims=True)
            hit = blk == first
            sels[gg] = jnp.where(hit & elig, 1.0, sels[gg])
            scores[gg] = jnp.where(hit, -3e38, scores[gg])
    for gg in range(2):
        sel_ref[:, gg * LANES:(gg + 1) * LANES] = sels[gg].astype(sel_ref.dtype)
    _merge_pair_heads(outs[0], outs[1], o_ref, tq)


def _nsa_compressed(z, kvc, tq=512):
    S = z.shape[0]
    n = S // CMP_STRIDE
    pool = (np.arange(n)[:, None] // (SEL_LEN // CMP_STRIDE) == np.arange(LANES)[None, :])
    pool = jnp.asarray(pool, BF16)
    return pl.pallas_call(
        functools.partial(_cmp_kernel, tq=tq),
        grid=(2, S // tq),
        in_specs=[
            pl.BlockSpec((tq, 512), lambda p, qi: (qi, C_NQ // 512 + p)),
            pl.BlockSpec((1, n, LANES), lambda p, qi: (0, 0, p)),
            pl.BlockSpec((1, n, LANES), lambda p, qi: (1, 0, p)),
            pl.BlockSpec((n, LANES), lambda p, qi: (0, 0)),
        ],
        out_specs=[
            pl.BlockSpec((tq, 512), lambda p, qi: (qi, p)),
            pl.BlockSpec((tq, 256), lambda p, qi: (qi, p)),
        ],
        out_shape=[
            jax.ShapeDtypeStruct((S, NSA_COLS), BF16),
            jax.ShapeDtypeStruct((S, 512), BF16),
        ],
        scratch_shapes=[pltpu.VMEM((2, NSA_HPG * tq, LANES), BF16)],
        compiler_params=_params(("parallel", "parallel")),
        name="nsa_cmp_topk",
    )(z, kvc, kvc, pool)


def _sel_kernel(q_ref, k_ref, v_ref, sel_ref, ex_ref, o_ref, qs_ref, m_ref, l_ref, acc_ref,
                *, tq, tk):
    qi = pl.program_id(1)
    _stack_group_queries(q_ref, qs_ref, tq)
    m_ref[...] = jnp.full(m_ref.shape, NEG, F32)
    l_ref[...] = jnp.zeros(l_ref.shape, F32)
    acc_ref[...] = jnp.zeros(acc_ref.shape, F32)
    row = qi * tq + lax.broadcasted_iota(jnp.int32, (tq, tk), 0)
    col = lax.broadcasted_iota(jnp.int32, (tq, tk), 1)

    def body(ki, carry):
        off = pl.multiple_of(ki * tk, tk)
        k = k_ref[pl.ds(off, tk), :]
        v = v_ref[pl.ds(off, tk), :]
        causal = off + col <= row
        for gg in range(2):
            hit = jnp.dot(sel_ref[:, gg * LANES:(gg + 1) * LANES], ex_ref[ki],
                          preferred_element_type=F32)
            bias = jnp.where(causal & (hit > 0.5), 0.0, NEG)
            s = lax.dot_general(qs_ref[gg], k, (((1,), (1,)), ((), ())),
                                preferred_element_type=F32)
            s = (s.reshape(NSA_HPG, tq, tk) + bias[None]).reshape(NSA_HPG * tq, tk)
            _online_softmax_step(s, v, m_ref.at[gg], l_ref.at[gg], acc_ref.at[gg])
        return carry

    lax.fori_loop(0, (qi * tq + tq - 1) // tk + 1, body, 0)
    _merge_pair_heads(acc_ref[0] / l_ref[0], acc_ref[1] / l_ref[1], o_ref, tq)


def _nsa_selected(z, sel, tq=256, tk=512):
    S = z.shape[0]
    tk = min(tk, S)
    expand = np.arange(LANES)[:, None] == np.arange(S)[None, :] // SEL_LEN
    expand = jnp.asarray(expand.reshape(LANES, S // tk, tk).transpose(1, 0, 2), BF16)
    return pl.pallas_call(
        functools.partial(_sel_kernel, tq=tq, tk=tk),
        grid=(2, S // tq),
        in_specs=[
            pl.BlockSpec((tq, 512), lambda p, qi: (qi, C_NQ // 512 + p)),
            pl.BlockSpec((S, LANES), lambda p, qi: (0, C_KSEL // LANES + p)),
            pl.BlockSpec((S, LANES), lambda p, qi: (0, C_VSEL // LANES + p)),
            pl.BlockSpec((tq, 256), lambda p, qi: (qi, p)),
            pl.BlockSpec((S // tk, LANES, tk), lambda p, qi: (0, 0, 0)),
        ],
        out_specs=pl.BlockSpec((tq, 512), lambda p, qi: (qi, p)),
        out_shape=jax.ShapeDtypeStruct((S, NSA_COLS), BF16),
        scratch_shapes=[
            pltpu.VMEM((2, NSA_HPG * tq, LANES), BF16),
            pltpu.VMEM((2, NSA_HPG * tq, LANES), F32),
            pltpu.VMEM((2, NSA_HPG * tq, LANES), F32),
            pltpu.VMEM((2, NSA_HPG * tq, LANES), F32),
        ],
        compiler_params=_params(("parallel", "arbitrary")),
        name="nsa_selected",
    )(z, z, z, sel, expand)


def _win_kernel(q_ref, *refs, tq, nprev):
    k_refs = refs[:nprev + 1]
    v_refs = refs[nprev + 1:2 * nprev + 2]
    o_ref, qs_ref = refs[2 * nprev + 2:]
    qi = pl.program_id(1)
    nk = (nprev + 1) * tq
    _stack_group_queries(q_ref, qs_ref, tq)
    k = jnp.concatenate([r[...] for r in k_refs], axis=0)
    v = jnp.concatenate([r[...] for r in v_refs], axis=0)
    t = qi * tq + lax.broadcasted_iota(jnp.int32, (tq, nk), 0)
    kpos = (qi - nprev) * tq + lax.broadcasted_iota(jnp.int32, (tq, nk), 1)
    dist = t - kpos
    bias = jnp.where((dist >= 0) & (dist < WINDOW) & (kpos >= 0), 0.0, NEG)[None]
    outs = []
    for gg in range(2):
        s = lax.dot_general(qs_ref[gg], k, (((1,), (1,)), ((), ())), preferred_element_type=F32)
        s = s.reshape(NSA_HPG, tq, nk) + bias
        m = jnp.max(s, axis=-1, keepdims=True)
        e = jnp.exp(s - m)
        p = e / jnp.sum(e, axis=-1, keepdims=True)
        outs.append(jnp.dot(p.reshape(NSA_HPG * tq, nk).astype(BF16), v,
                            preferred_element_type=F32))
    _merge_pair_heads(outs[0], outs[1], o_ref, tq)


def _nsa_window(z, tq=256):
    S = z.shape[0]
    nprev = WINDOW // tq

    def kv_spec(off, back):
        return pl.BlockSpec((tq, LANES), lambda p, qi: (jnp.maximum(qi - back, 0), off + p))

    backs = list(range(nprev, -1, -1))
    return pl.pallas_call(
        functools.partial(_win_kernel, tq=tq, nprev=nprev),
        grid=(2, S // tq),
        in_specs=[pl.BlockSpec((tq, 512), lambda p, qi: (qi, C_NQ // 512 + p))]
        + [kv_spec(C_KWIN // LANES, b) for b in backs]
        + [kv_spec(C_VWIN // LANES, b) for b in backs],
        out_specs=pl.BlockSpec((tq, 512), lambda p, qi: (qi, p)),
        out_shape=jax.ShapeDtypeStruct((S, NSA_COLS), BF16),
        scratch_shapes=[pltpu.VMEM((2, NSA_HPG * tq, LANES), BF16)],
        compiler_params=_params(("parallel", "parallel")),
        name="nsa_window",
    )(z, *([z] * (2 * nprev + 2)))


def _memkv_kernel(mem_ref, g_ref, w_ref, o_ref):
    memn = _rms(mem_ref[...], g_ref[...]).astype(BF16)
    o_ref[...] = jnp.dot(memn, w_ref[...], preferred_element_type=F32).astype(o_ref.dtype)


def _memkv(mem, g, w):
    M = mem.shape[0]
    return pl.pallas_call(
        _memkv_kernel,
        out_shape=jax.ShapeDtypeStruct((M, 2 * X_HEADS * X_DH), BF16),
        compiler_params=pltpu.CompilerParams(vmem_limit_bytes=VMEM_LIMIT),
        name="mem_kv",
    )(mem, g, w)


def _mid_kernel(x_ref, oda_ref, oc_ref, os_ref, ow_ref, ng_ref, bg0_ref, bg1_ref, eg_ref,
                wda_ref, wnsa_ref, wout_ref, gx_ref, wxq_ref, mkv_ref, wxo_ref, gf_ref,
                h_ref, xnt_ref):
    ng = ng_ref[...]
    o_nsa = jnp.zeros(oc_ref.shape, F32)
    for c, br in enumerate((oc_ref, os_ref, ow_ref)):
        gate = jax.nn.sigmoid(jnp.dot(ng, eg_ref[c], preferred_element_type=F32))
        o_nsa = o_nsa + gate * br[...].astype(F32)
    y_da = jnp.dot(oda_ref[...], wda_ref[...], preferred_element_type=F32)
    y_nsa = jnp.dot(o_nsa.astype(BF16), wnsa_ref[...], preferred_element_type=F32)
    mix = (jax.nn.sigmoid(bg0_ref[...].astype(F32)) * y_da
           + jax.nn.sigmoid(bg1_ref[...].astype(F32)) * y_nsa)
    h = x_ref[...] + jnp.dot(mix.astype(BF16), wout_ref[...], preferred_element_type=F32)
    hn = _rms(h, gx_ref[...]).astype(BF16)
    q = (jnp.dot(hn, wxq_ref[...], preferred_element_type=F32) * (X_DH ** -0.5)).astype(BF16)
    outs = []
    for hd in range(X_HEADS):
        k = mkv_ref[:, hd * X_DH:(hd + 1) * X_DH]
        v = mkv_ref[:, (X_HEADS + hd) * X_DH:(X_HEADS + hd + 1) * X_DH]
        s = lax.dot_general(q[:, hd * X_DH:(hd + 1) * X_DH], k, (((1,), (1,)), ((), ())),
                            preferred_element_type=F32)
        e = jnp.exp(s - jnp.max(s, axis=-1, keepdims=True))
        p = (e / jnp.sum(e, axis=-1, keepdims=True)).astype(BF16)
        outs.append(jnp.dot(p, v, preferred_element_type=F32))
    o = jnp.concatenate(outs, axis=1).astype(BF16)
    h = h + jnp.dot(o, wxo_ref[...], preferred_element_type=F32)
    h_ref[...] = h
    xnt_ref[...] = _rms(h, gf_ref[...]).T.astype(BF16)


def _mid(x, z, o_da, o_c, o_s, o_w, eg, wda, wnsa, wout, gx, wxq, mkv, wxo, gf, tm=256):
    S = x.shape[0]
    M = mkv.shape[0]
    row = lambda w: pl.BlockSpec((tm, w), lambda i: (i, 0))
    return pl.pallas_call(
        _mid_kernel,
        grid=(S // tm,),
        in_specs=[
            row(D_MODEL), row(1024), row(NSA_COLS), row(NSA_COLS), row(NSA_COLS),
            pl.BlockSpec((tm, LANES), lambda i: (i, C_NGB // LANES)),
            pl.BlockSpec((tm, D_MODEL), lambda i: (i, C_BG // D_MODEL)),
            pl.BlockSpec((tm, D_MODEL), lambda i: (i, C_BG // D_MODEL + 1)),
            _const_spec((3, LANES, NSA_COLS)),
            _const_spec((1024, D_MODEL)), _const_spec((NSA_COLS, D_MODEL)),
            _const_spec((D_MODEL, D_MODEL)),
            _const_spec((1, D_MODEL)), _const_spec((D_MODEL, X_HEADS * X_DH)),
            _const_spec((M, 2 * X_HEADS * X_DH)), _const_spec((X_HEADS * X_DH, D_MODEL)),
            _const_spec((1, D_MODEL)),
        ],
        out_specs=[
            pl.BlockSpec((tm, D_MODEL), lambda i: (i, 0)),
            pl.BlockSpec((D_MODEL, tm), lambda i: (0, i)),
        ],
        out_shape=[
            jax.ShapeDtypeStruct((S, D_MODEL), F32),
            jax.ShapeDtypeStruct((D_MODEL, S), BF16),
        ],
        compiler_params=_params(("parallel",)),
        name="merge_xattn",
    )(x, o_da, o_c, o_s, o_w, z, z, z, eg, wda, wnsa, wout, gx, wxq, mkv, wxo, gf)


def _gate_expand():
    e = np.zeros((3, LANES, NSA_COLS), np.float32)
    for c in range(3):
        for g in range(NSA_GROUPS):
            for h in range(NSA_HPG):
                col = ((g // 2) * NSA_HPG + h) * LANES + (g % 2) * NSA_DK
                e[c, NG_LANE0 + g * 12 + h * 3 + c, col:col + NSA_DK] = 1.0
    return jnp.asarray(e, BF16)


def _topk_rows(x, k):
    vals = []
    for _ in range(k):
        m = jnp.max(x, axis=0, keepdims=True)
        vals.append(m)
        x = jnp.where(x == m, -3e38, x)
    return vals


def _peer_pre_kernel(xnt_ref, wqt_ref, k1_ref, k2_ref, t1_ref, s2_ref, e1_ref, e2_ref):
    half = PEER_DQ // 2
    qt = jnp.dot(wqt_ref[...], xnt_ref[...], preferred_element_type=F32).astype(BF16)
    for h in range(PEER_HEADS):
        s1 = jnp.dot(k1_ref[...], qt[h * PEER_DQ:h * PEER_DQ + half], preferred_element_type=F32)
        s2 = jnp.dot(k2_ref[...], qt[h * PEER_DQ + half:(h + 1) * PEER_DQ],
                     preferred_element_type=F32)
        v1 = _topk_rows(s1, PEER_TOPK + 1)
        v2r = _topk_rows(s2, PEER_TOPK + 1)
        v2 = jnp.concatenate(v2r[:PEER_TOPK], axis=0)
        cand = jnp.concatenate([a + v2 for a in v1[:PEER_TOPK]], axis=0)
        top = _topk_rows(cand, PEER_TOPK + 1)
        nxt = jnp.maximum(top[PEER_TOPK], jnp.maximum(v1[PEER_TOPK] + v2r[0], v1[0] + v2r[PEER_TOPK]))
        thr = 0.5 * (top[PEER_TOPK - 1] + nxt)
        zsum = jnp.sum(jnp.where(cand > thr, jnp.exp(cand - top[0]), 0.0), axis=0, keepdims=True)
        t1_ref[h] = thr - s1
        s2_ref[h] = s2
        e1_ref[h] = jnp.exp(s1 - v1[0])
        e2_ref[h] = jnp.exp(s2 - v2[0:1]) / zsum


def _peer_pre(xnt, wqt, k1, k2, tt=512):
    S = xnt.shape[1]
    big = lambda: pl.BlockSpec((PEER_HEADS, PEER_NKEYS, tt), lambda i: (0, 0, i))
    sds = jax.ShapeDtypeStruct((PEER_HEADS, PEER_NKEYS, S), F32)
    return pl.pallas_call(
        _peer_pre_kernel,
        grid=(S // tt,),
        in_specs=[
            pl.BlockSpec((D_MODEL, tt), lambda i: (0, i)),
            _const_spec((PEER_HEADS * PEER_DQ, D_MODEL)),
            _const_spec((PEER_NKEYS, PEER_DQ // 2)),
            _const_spec((PEER_NKEYS, PEER_DQ // 2)),
        ],
        out_specs=[big(), big(), big(), big()],
        out_shape=[sds, sds, sds, sds],
        compiler_params=_params(("parallel",)),
        name="peer_scores",
    )(xnt, wqt, k1, k2)


PEER_JB = 32


def _peer_kernel(xnt_ref, u_ref, vt_ref, t1_ref, s2_ref, e1_ref, e2_ref, o_ref, a_ref, wg_ref,
                 wgp_ref, *, ic):
    e = pl.program_id(1)
    ne = pl.num_programs(1) - 1
    tt = xnt_ref.shape[1]

    @pl.when(e == 0)
    def _():
        o_ref[...] = jnp.zeros(o_ref.shape, F32)
        wg_ref[...] = jnp.zeros(wg_ref.shape, BF16)

    nblk = PEER_NKEYS // PEER_JB

    def apply_previous():
        wgp_ref[...] = wg_ref[...]
        o_ref[...] += jnp.dot(vt_ref[...], wgp_ref[...], preferred_element_type=F32)

    def build_gates(ii):
        assert 8 % ic == 0
        grp = pl.ds(pl.multiple_of((e * ic) // 8 * 8, 8), 8)
        sub = (e * ic) % 8

        def key_row(ref, h, cols):
            rows8 = ref[h, grp, cols]
            row = rows8[ii:ii + 1]
            for k in range(1, 8 // ic):
                row = jnp.where(sub == k * ic, rows8[k * ic + ii:k * ic + ii + 1], row)
            return row

        for c in range(tt // LANES):
            cols = slice(c * LANES, (c + 1) * LANES)
            taus = [key_row(t1_ref, h, cols) for h in range(PEER_HEADS)]
            e1s = [key_row(e1_ref, h, cols) for h in range(PEER_HEADS)]
            for jb in range(nblk):
                j = slice(jb * PEER_JB, (jb + 1) * PEER_JB)
                rows = slice(ii * PEER_NKEYS + jb * PEER_JB, ii * PEER_NKEYS + (jb + 1) * PEER_JB)
                w = jnp.zeros((PEER_JB, LANES), F32)
                for h in range(PEER_HEADS):
                    w = w + jnp.where(s2_ref[h, j, cols] > taus[h],
                                      e2_ref[h, j, cols] * e1s[h], 0.0)
                wg_ref[rows, cols] = (_gelu(a_ref[rows, cols]) * w).astype(BF16)

    @pl.when(e < ne)
    def _():
        a_ref[...] = jnp.dot(u_ref[...], xnt_ref[...], preferred_element_type=F32)
        apply_previous()
        for ii in range(ic):
            build_gates(ii)

    @pl.when(e == ne)
    def _():
        apply_previous()


def _peer(xnt, u, vt, t1, s2, e1, e2, tt=512, ic=4):
    S = xnt.shape[1]
    ec = ic * PEER_NKEYS
    ne = (PEER_NKEYS * PEER_NKEYS) // ec
    tok = pl.BlockSpec((PEER_HEADS, PEER_NKEYS, tt), lambda i, e: (0, 0, i))
    return pl.pallas_call(
        functools.partial(_peer_kernel, ic=ic),
        grid=(S // tt, ne + 1),
        in_specs=[
            pl.BlockSpec((D_MODEL, tt), lambda i, e: (0, i)),
            pl.BlockSpec((ec, D_MODEL), lambda i, e: (jnp.minimum(e, ne - 1), 0)),
            pl.BlockSpec((D_MODEL, ec), lambda i, e: (0, jnp.maximum(e - 1, 0))),
            tok, tok, tok, tok,
        ],
        out_specs=pl.BlockSpec((D_MODEL, tt), lambda i, e: (0, i)),
        out_shape=jax.ShapeDtypeStruct((D_MODEL, S), F32),
        scratch_shapes=[pltpu.VMEM((ec, tt), F32), pltpu.VMEM((ec, tt), BF16),
                        pltpu.VMEM((ec, tt), BF16)],
        compiler_params=_params(("parallel", "arbitrary")),
        name="peer_experts",
    )(xnt, u, vt, t1, s2, e1, e2)


def _final_kernel(h_ref, yt_ref, g_ref, o_ref):
    o_ref[...] = _rms(h_ref[...] + yt_ref[...].T, g_ref[...])


def _final(h, yt, g, tm=256):
    S = h.shape[0]
    return pl.pallas_call(
        _final_kernel,
        grid=(S // tm,),
        in_specs=[
            pl.BlockSpec((tm, D_MODEL), lambda i: (i, 0)),
            pl.BlockSpec((D_MODEL, tm), lambda i: (0, i)),
            pl.BlockSpec((1, D_MODEL), lambda i: (0, 0)),
        ],
        out_specs=pl.BlockSpec((tm, D_MODEL), lambda i: (i, 0)),
        out_shape=jax.ShapeDtypeStruct((S, D_MODEL), F32),
        compiler_params=_params(("parallel",)),
        name="final_norm",
    )(h, yt, g)


def kernel(x, mem, positions, norm_mix_g, w_in, da_lambda_q1, da_lambda_k1, da_lambda_q2, da_lambda_k2, da_subln_g, nsa_cmp_pos, nsa_cmpk_w1, nsa_cmpk_w2, nsa_cmpv_w1, nsa_cmpv_w2, w_branch_da, w_branch_nsa, w_out, norm_x_g, norm_mem_g, w_xq, w_xkv, w_xo, norm_ffn_g, peer_wq, peer_keys1, peer_keys2, peer_u, peer_v, norm_final_g):
    B, S, D = x.shape
    assert B == 1 and D == D_MODEL and norm_mix_g.shape[0] == 1
    l = 0
    lambda_init = 0.8 - 0.6 * math.exp(-0.3 * l)
    lam = (jnp.exp(jnp.sum((da_lambda_q1[l] * da_lambda_k1[l]).astype(F32)))
           - jnp.exp(jnp.sum((da_lambda_q2[l] * da_lambda_k2[l]).astype(F32))) + lambda_init)
    lam_row = jnp.full((1, LANES), lam, F32)

    z = _inproj(x[0], norm_mix_g[l][None], _pack_w_in(w_in[l]), *_rope_tables(positions[0]))
    o_da = _diff_attention(z, lam_row, da_subln_g[l][None], lambda_init)
    kvc = _compress(z, nsa_cmp_pos[l], jnp.stack([nsa_cmpk_w1[l], nsa_cmpv_w1[l]]),
                    jnp.stack([nsa_cmpk_w2[l], nsa_cmpv_w2[l]]))
    o_c, sel = _nsa_compressed(z, kvc)
    o_s = _nsa_selected(z, sel)
    o_w = _nsa_window(z)

    wnsa = w_branch_nsa[l].reshape(2, 2, NSA_HPG, NSA_DK, D).transpose(0, 2, 1, 3, 4)
    wnsa = wnsa.reshape(NSA_COLS, D).astype(BF16)
    mkv = _memkv(mem[0], norm_mem_g[l][None], w_xkv[l].astype(BF16))
    h, xnt = _mid(x[0], z, o_da, o_c, o_s, o_w, _gate_expand(), w_branch_da[l].astype(BF16), wnsa,
                  w_out[l].astype(BF16), norm_x_g[l][None], w_xq[l].astype(BF16), mkv,
                  w_xo[l].astype(BF16), norm_ffn_g[l][None])

    t1, s2, e1, e2 = _peer_pre(xnt, peer_wq[l].T.astype(BF16), peer_keys1[l].astype(BF16),
                               peer_keys2[l].astype(BF16))
    yt = _peer(xnt, peer_u[l].astype(BF16), peer_v[l].T.astype(BF16), t1, s2, e1, e2)
    return _final(h, yt, norm_final_g[None])[None]
```

```python
import functools
import math

import jax
import jax.numpy as jnp
import numpy as np
from jax import lax
from jax.experimental import pallas as pl
from jax.experimental.pallas import tpu as pltpu

F32 = jnp.float32
BF16 = jnp.bfloat16

D_MODEL = 2048
EPS = 1e-6
NEG = -1e30
ROPE_THETA = 500000.0
DA_HEADS = 8
DA_DK = 64
DA_DV = 128
NSA_GROUPS = 4
NSA_HPG = 4
NSA_DK = 64
CMP_LEN = 32
CMP_STRIDE = 16
CMP_HIDDEN = 128
SEL_LEN = 64
SEL_TOPK = 16
WINDOW = 512
FORCE_BONUS = 100.0
ROPE_DIM = 16
X_HEADS = 4
X_DH = 128
PEER_HEADS = 8
PEER_NKEYS = 128
PEER_DQ = 256
PEER_TOPK = 16

LANES = 128
VMEM_LIMIT = 56 * 1024 * 1024

C_ZQ = 0
C_ZK = 1024
C_NQ = 2048
C_KCMP = 3072
C_KSEL = 3328
C_KWIN = 3584
C_NGB = 3840
NG_LANE0 = 16
N_ROPE_COLS = 4096
C_ZV = 4096
C_VCMP = 5120
C_VSEL = 5376
C_VWIN = 5632
C_BG = 6144
N_COLS = 10240
TN_IN = 1024
NSA_COLS = 1024


def _params(sem):
    return pltpu.CompilerParams(dimension_semantics=sem, vmem_limit_bytes=VMEM_LIMIT)


def _const_spec(shape):
    n = len(shape)
    return pl.BlockSpec(shape, lambda *_: (0,) * n, pipeline_mode=pl.Buffered(1))


def _gelu(x):
    return 0.5 * x * (1.0 + lax.erf(x * (2.0 ** -0.5)))


def _rms(x, g):
    ms = jnp.mean(x * x, axis=-1, keepdims=True)
    return x * lax.rsqrt(ms + EPS) * g


def _inproj_kernel(x_ref, g_ref, w_ref, c_ref, sp_ref, sm_ref, o_ref, xn_ref):
    j = pl.program_id(1)

    @pl.when(j == 0)
    def _():
        xn_ref[...] = _rms(x_ref[...], g_ref[...]).astype(BF16)

    z = jnp.dot(xn_ref[...], w_ref[...], preferred_element_type=F32)

    @pl.when(j < N_ROPE_COLS // TN_IN)
    def _():
        c = c_ref[...]
        sp = sp_ref[...]
        sm = sm_ref[...]
        for k in range(TN_IN // LANES):
            zc = z[:, k * LANES:(k + 1) * LANES]
            r = zc * c + pltpu.roll(zc, 8, 1) * sp + pltpu.roll(zc, LANES - 8, 1) * sm
            o_ref[:, k * LANES:(k + 1) * LANES] = r.astype(o_ref.dtype)

    @pl.when(j >= N_ROPE_COLS // TN_IN)
    def _():
        o_ref[...] = z.astype(o_ref.dtype)


def _inproj(x, g, w, rope_c, rope_sp, rope_sm, tm=512):
    S = x.shape[0]
    return pl.pallas_call(
        _inproj_kernel,
        grid=(S // tm, N_COLS // TN_IN),
        in_specs=[
            pl.BlockSpec((tm, D_MODEL), lambda i, j: (i, 0)),
            pl.BlockSpec((1, D_MODEL), lambda i, j: (0, 0)),
            pl.BlockSpec((D_MODEL, TN_IN), lambda i, j: (0, j)),
            pl.BlockSpec((tm, LANES), lambda i, j: (i, 0)),
            pl.BlockSpec((tm, LANES), lambda i, j: (i, 0)),
            pl.BlockSpec((tm, LANES), lambda i, j: (i, 0)),
        ],
        out_specs=pl.BlockSpec((tm, TN_IN), lambda i, j: (i, j)),
        out_shape=jax.ShapeDtypeStruct((S, N_COLS), BF16),
        scratch_shapes=[pltpu.VMEM((tm, D_MODEL), BF16)],
        compiler_params=_params(("parallel", "arbitrary")),
        name="inproj",
    )(x, g, w, rope_c, rope_sp, rope_sm)


def _pack_w_in(w_in):
    zq, zk, zv, nq, nkv, ng, bg = jnp.split(
        w_in.astype(BF16), np.cumsum([1024, 1024, 1024, 1024, 1536, 48]).tolist(), axis=1)
    D = w_in.shape[0]
    nq = nq.reshape(D, 2, 2, NSA_HPG, NSA_DK).transpose(0, 1, 3, 2, 4).reshape(D, 1024)
    nkv = nkv.reshape(D, 6, 256)
    ngb = jnp.concatenate([jnp.zeros((D, NG_LANE0), BF16), ng,
                           jnp.zeros((D, 256 - NG_LANE0 - 48), BF16)], axis=1)
    pad = jnp.zeros((D, C_BG - C_VWIN - 256), BF16)
    return jnp.concatenate([zq, zk, nq, nkv[:, 0], nkv[:, 2], nkv[:, 4], ngb,
                            zv, nkv[:, 1], nkv[:, 3], nkv[:, 5], pad, bg], axis=1)


def _rope_tables(positions):
    inv = ROPE_THETA ** (-jnp.arange(0, ROPE_DIM, 2, dtype=F32) / ROPE_DIM)
    ang = positions.astype(F32)[:, None] * inv
    cos, sin = jnp.cos(ang), jnp.sin(ang)
    S = positions.shape[0]
    one = jnp.ones((S, 48), F32)
    zero = jnp.zeros((S, 48), F32)
    z8 = jnp.zeros((S, 8), F32)
    c64 = jnp.concatenate([cos, cos, one], axis=1)
    sp64 = jnp.concatenate([z8, sin, zero], axis=1)
    sm64 = jnp.concatenate([-sin, z8, zero], axis=1)
    return (jnp.tile(c64, (1, 2)), jnp.tile(sp64, (1, 2)), jnp.tile(sm64, (1, 2)))


def _online_softmax_step(s, v, m_ref, l_ref, acc_ref):
    m_prev = m_ref[...]
    m_new = jnp.maximum(m_prev, jnp.max(s, axis=-1, keepdims=True))
    alpha = jnp.exp(m_prev - m_new)
    p = jnp.exp(s - m_new[:, 0:1])
    l_ref[...] = alpha * l_ref[...] + jnp.sum(p, axis=-1, keepdims=True)
    acc_ref[...] = alpha * acc_ref[...] + jnp.dot(p.astype(BF16), v, preferred_element_type=F32)
    m_ref[...] = m_new


DA_SPLIT = 2
DA_WIDE = 4


def _da_kernel(lam_ref, q_ref, k_ref, v_ref, g_ref, o_ref, qs_ref, m_ref, l_ref, acc_ref,
               *, tq, out_scale):
    qi = pl.program_id(1)
    q = q_ref[...].astype(F32) * (DA_DK ** -0.5)
    lane = lax.broadcasted_iota(jnp.int32, q.shape, 1)
    qs_ref[0:tq, :] = jnp.where(lane < DA_DK, q, 0.0).astype(BF16)
    qs_ref[tq:2 * tq, :] = jnp.where(lane >= DA_DK, q, 0.0).astype(BF16)
    m_ref[...] = jnp.full(m_ref.shape, NEG, F32)
    l_ref[...] = jnp.zeros(l_ref.shape, F32)
    acc_ref[...] = jnp.zeros(acc_ref.shape, F32)

    rc = tq // DA_SPLIT

    def tile(off, width, masked):
        off = pl.multiple_of(off, tq)
        k = k_ref[pl.ds(off, width), :]
        v = v_ref[pl.ds(off, width), :]
        for c in range(2 * DA_SPLIT):
            r = slice(c * rc, (c + 1) * rc)
            s = lax.dot_general(qs_ref[r, :], k, (((1,), (1,)), ((), ())),
                                preferred_element_type=F32)
            if masked:
                row = (c % DA_SPLIT) * rc + lax.broadcasted_iota(jnp.int32, (rc, width), 0)
                col = lax.broadcasted_iota(jnp.int32, (rc, width), 1)
                s = jnp.where(col <= row + (width - tq), s, NEG)
            _online_softmax_step(s, v, m_ref.at[r], l_ref.at[r], acc_ref.at[r])

    def wide(kw, carry):
        tile(kw * (DA_WIDE * tq), DA_WIDE * tq, False)
        return carry

    nw = qi // DA_WIDE
    lax.fori_loop(0, nw, wide, 0)
    rem = qi - nw * DA_WIDE
    for r_ in range(DA_WIDE):
        @pl.when(rem == r_)
        def _(r_=r_):
            tile(nw * (DA_WIDE * tq), (r_ + 1) * tq, True)

    o = acc_ref[...] / l_ref[...]
    d = o[0:tq] - lam_ref[...] * o[tq:2 * tq]
    o_ref[...] = (_rms(d, g_ref[...]) * out_scale).astype(o_ref.dtype)


def _diff_attention(z, lam_row, subln_g, lambda_init, tq=512):
    S = z.shape[0]
    return pl.pallas_call(
        functools.partial(_da_kernel, tq=tq, out_scale=1.0 - lambda_init),
        grid=(DA_HEADS, S // tq),
        in_specs=[
            pl.BlockSpec((1, LANES), lambda h, qi: (0, 0)),
            pl.BlockSpec((tq, LANES), lambda h, qi: (qi, C_ZQ // LANES + h)),
            pl.BlockSpec((S, LANES), lambda h, qi: (0, C_ZK // LANES + h)),
            pl.BlockSpec((S, LANES), lambda h, qi: (0, C_ZV // LANES + h)),
            pl.BlockSpec((1, LANES), lambda h, qi: (0, 0)),
        ],
        out_specs=pl.BlockSpec((tq, LANES), lambda h, qi: (qi, h)),
        out_shape=jax.ShapeDtypeStruct((S, DA_HEADS * DA_DV), BF16),
        scratch_shapes=[
            pltpu.VMEM((2 * tq, LANES), BF16),
            pltpu.VMEM((2 * tq, LANES), F32),
            pltpu.VMEM((2 * tq, LANES), F32),
            pltpu.VMEM((2 * tq, LANES), F32),
        ],
        compiler_params=_params(("parallel", "arbitrary")),
        name="diff_attn",
    )(lam_row, z, z, z, subln_g)


def _compress_kernel(x_ref, pos_ref, w1_ref, w2_ref, o_ref):
    n = x_ref.shape[3]
    w1a = w1_ref[0, 0:CMP_STRIDE * NSA_DK, :]
    w1b = w1_ref[0, CMP_STRIDE * NSA_DK:, :]
    c = jnp.dot(pos_ref[...], w1_ref[0], preferred_element_type=F32)[0:1]
    outs = []
    for gg in range(2):
        xs = x_ref[0, 0, gg]
        y1 = jnp.dot(xs, w1a, preferred_element_type=F32)
        y2 = jnp.dot(xs, w1b, preferred_element_type=F32)
        hid = _gelu(y1 + pltpu.roll(y2, n - 1, 0) + c)
        outs.append(jnp.dot(hid.astype(BF16), w2_ref[0], preferred_element_type=F32))
    o_ref[0] = jnp.concatenate(outs, axis=1).astype(o_ref.dtype)


def _compress(z, pos, w1, w2):
    S = z.shape[0]
    n = S // CMP_STRIDE

    def blocks(c0):
        a = z[:, c0:c0 + 256].reshape(n, CMP_STRIDE, 2, 2, NSA_DK)
        return a.transpose(2, 3, 0, 1, 4).reshape(2, 2, n, CMP_STRIDE * NSA_DK)

    xs = jnp.stack([blocks(C_KCMP), blocks(C_VCMP)])
    pos8 = jnp.broadcast_to(pos.reshape(1, CMP_LEN * NSA_DK), (8, CMP_LEN * NSA_DK)).astype(BF16)
    return pl.pallas_call(
        _compress_kernel,
        grid=(2, 2),
        in_specs=[
            pl.BlockSpec((1, 1, 2, n, CMP_STRIDE * NSA_DK), lambda c, p: (c, p, 0, 0, 0)),
            pl.BlockSpec((8, CMP_LEN * NSA_DK), lambda c, p: (0, 0)),
            pl.BlockSpec((1, CMP_LEN * NSA_DK, CMP_HIDDEN), lambda c, p: (c, 0, 0)),
            pl.BlockSpec((1, CMP_HIDDEN, NSA_DK), lambda c, p: (c, 0, 0)),
        ],
        out_specs=pl.BlockSpec((1, n, LANES), lambda c, p: (c, 0, p)),
        out_shape=jax.ShapeDtypeStruct((2, n, 256), BF16),
        compiler_params=_params(("parallel", "parallel")),
        name="nsa_compress",
    )(xs, pos8, w1.astype(BF16), w2.astype(BF16))


def _stack_group_queries(q_ref, qs_ref, tq):
    lane = lax.broadcasted_iota(jnp.int32, (tq, LANES), 1)
    for h in range(NSA_HPG):
        q = q_ref[:, h * LANES:(h + 1) * LANES].astype(F32) * (NSA_DK ** -0.5)
        qs_ref[0, h * tq:(h + 1) * tq, :] = jnp.where(lane < NSA_DK, q, 0.0).astype(BF16)
        qs_ref[1, h * tq:(h + 1) * tq, :] = jnp.where(lane >= NSA_DK, q, 0.0).astype(BF16)


def _merge_pair_heads(o0, o1, o_ref, tq):
    lane = lax.broadcasted_iota(jnp.int32, (tq, LANES), 1)
    for h in range(NSA_HPG):
        o_ref[:, h * LANES:(h + 1) * LANES] = jnp.where(
            lane < NSA_DK, o0[h * tq:(h + 1) * tq], o1[h * tq:(h + 1) * tq]).astype(o_ref.dtype)


def _cmp_kernel(q_ref, kc_ref, vc_ref, pool_ref, o_ref, sel_ref, qs_ref, *, tq):
    qi = pl.program_id(1)
    n = kc_ref.shape[1]
    _stack_group_queries(q_ref, qs_ref, tq)
    t = qi * tq + lax.broadcasted_iota(jnp.int32, (tq, n), 0)
    cmp_end = lax.broadcasted_iota(jnp.int32, (tq, n), 1) * CMP_STRIDE + (CMP_LEN - 1)
    vis = (cmp_end <= t)[None]
    tb = qi * tq + lax.broadcasted_iota(jnp.int32, (tq, LANES), 0)
    blk = lax.broadcasted_iota(jnp.int32, (tq, LANES), 1)
    elig = blk * SEL_LEN <= tb
    cur = tb // SEL_LEN
    forced = (blk == 0) | (blk == cur) | (blk == cur - 1)
    outs, scores = [], []
    for gg in range(2):
        s = lax.dot_general(qs_ref[gg], kc_ref[0], (((1,), (1,)), ((), ())),
                            preferred_element_type=F32).reshape(NSA_HPG, tq, n)
        s = jnp.where(vis, s, NEG)
        m = jnp.max(s, axis=-1, keepdims=True)
        e = jnp.where(vis, jnp.exp(s - m), 0.0)
        den = jnp.sum(e, axis=-1, keepdims=True)
        p = e / jnp.where(den == 0.0, 1.0, den)
        outs.append(jnp.dot(p.reshape(NSA_HPG * tq, n).astype(BF16), vc_ref[0],
                            preferred_element_type=F32))
        imp = jnp.sum(p, axis=0)
        hi = imp.astype(BF16)
        r1 = imp - hi.astype(F32)
        mid = r1.astype(BF16)
        lo = (r1 - mid.astype(F32)).astype(BF16)
        pooled = (jnp.dot(hi, pool_ref[...], preferred_element_type=F32)
                  + jnp.dot(mid, pool_ref[...], preferred_element_type=F32)
                  + jnp.dot(lo, pool_ref[...], preferred_element_type=F32))
        scores.append(jnp.where(elig, pooled + jnp.where(forced, FORCE_BONUS, 0.0), -1.0))
    sels = [jnp.zeros((tq, LANES), F32), jnp.zeros((tq, LANES), F32)]
    for _ in range(SEL_TOPK):
        for gg in range(2):
            mx = jnp.max(scores[gg], axis=-1, keepdims=True)
            first = jnp.min(jnp.where(scores[gg] == mx, blk, LANES), axis=-1, keepdims=True)
            hit = blk == first
            sels[gg] = jnp.where(hit & elig, 1.0, sels[gg])
            scores[gg] = jnp.where(hit, -3e38, scores[gg])
    for gg in range(2):
        sel_ref[:, gg * LANES:(gg + 1) * LANES] = sels[gg].astype(sel_ref.dtype)
    _merge_pair_heads(outs[0], outs[1], o_ref, tq)


def _nsa_compressed(z, kvc, tq=512):
    S = z.shape[0]
    n = S // CMP_STRIDE
    pool = (np.arange(n)[:, None] // (SEL_LEN // CMP_STRIDE) == np.arange(LANES)[None, :])
    pool = jnp.asarray(pool, BF16)
    return pl.pallas_call(
        functools.partial(_cmp_kernel, tq=tq),
        grid=(2, S // tq),
        in_specs=[
            pl.BlockSpec((tq, 512), lambda p, qi: (qi, C_NQ // 512 + p)),
            pl.BlockSpec((1, n, LANES), lambda p, qi: (0, 0, p)),
            pl.BlockSpec((1, n, LANES), lambda p, qi: (1, 0, p)),
            pl.BlockSpec((n, LANES), lambda p, qi: (0, 0)),
        ],
        out_specs=[
            pl.BlockSpec((tq, 512), lambda p, qi: (qi, p)),
            pl.BlockSpec((tq, 256), lambda p, qi: (qi, p)),
        ],
        out_shape=[
            jax.ShapeDtypeStruct((S, NSA_COLS), BF16),
            jax.ShapeDtypeStruct((S, 512), BF16),
        ],
        scratch_shapes=[pltpu.VMEM((2, NSA_HPG * tq, LANES), BF16)],
        compiler_params=_params(("parallel", "parallel")),
        name="nsa_cmp_topk",
    )(z, kvc, kvc, pool)


def _sel_kernel(q_ref, k_ref, v_ref, sel_ref, ex_ref, o_ref, qs_ref, m_ref, l_ref, acc_ref,
                *, tq, tk):
    qi = pl.program_id(1)
    _stack_group_queries(q_ref, qs_ref, tq)
    m_ref[...] = jnp.full(m_ref.shape, NEG, F32)
    l_ref[...] = jnp.zeros(l_ref.shape, F32)
    acc_ref[...] = jnp.zeros(acc_ref.shape, F32)
    row = qi * tq + lax.broadcasted_iota(jnp.int32, (tq, tk), 0)
    col = lax.broadcasted_iota(jnp.int32, (tq, tk), 1)

    def body(ki, carry):
        off = pl.multiple_of(ki * tk, tk)
        k = k_ref[pl.ds(off, tk), :]
        v = v_ref[pl.ds(off, tk), :]
        causal = off + col <= row
        for gg in range(2):
            hit = jnp.dot(sel_ref[:, gg * LANES:(gg + 1) * LANES], ex_ref[ki],
                          preferred_element_type=F32)
            bias = jnp.where(causal & (hit > 0.5), 0.0, NEG)
            s = lax.dot_general(qs_ref[gg], k, (((1,), (1,)), ((), ())),
                                preferred_element_type=F32)
            s = (s.reshape(NSA_HPG, tq, tk) + bias[None]).reshape(NSA_HPG * tq, tk)
            _online_softmax_step(s, v, m_ref.at[gg], l_ref.at[gg], acc_ref.at[gg])
        return carry

    lax.fori_loop(0, (qi * tq + tq - 1) // tk + 1, body, 0)
    _merge_pair_heads(acc_ref[0] / l_ref[0], acc_ref[1] / l_ref[1], o_ref, tq)


def _nsa_selected(z, sel, tq=256, tk=512):
    S = z.shape[0]
    tk = min(tk, S)
    expand = np.arange(LANES)[:, None] == np.arange(S)[None, :] // SEL_LEN
    expand = jnp.asarray(expand.reshape(LANES, S // tk, tk).transpose(1, 0, 2), BF16)
    return pl.pallas_call(
        functools.partial(_sel_kernel, tq=tq, tk=tk),
        grid=(2, S // tq),
        in_specs=[
            pl.BlockSpec((tq, 512), lambda p, qi: (qi, C_NQ // 512 + p)),
            pl.BlockSpec((S, LANES), lambda p, qi: (0, C_KSEL // LANES + p)),
            pl.BlockSpec((S, LANES), lambda p, qi: (0, C_VSEL // LANES + p)),
            pl.BlockSpec((tq, 256), lambda p, qi: (qi, p)),
            pl.BlockSpec((S // tk, LANES, tk), lambda p, qi: (0, 0, 0)),
        ],
        out_specs=pl.BlockSpec((tq, 512), lambda p, qi: (qi, p)),
        out_shape=jax.ShapeDtypeStruct((S, NSA_COLS), BF16),
        scratch_shapes=[
            pltpu.VMEM((2, NSA_HPG * tq, LANES), BF16),
            pltpu.VMEM((2, NSA_HPG * tq, LANES), F32),
            pltpu.VMEM((2, NSA_HPG * tq, LANES), F32),
            pltpu.VMEM((2, NSA_HPG * tq, LANES), F32),
        ],
        compiler_params=_params(("parallel", "arbitrary")),
        name="nsa_selected",
    )(z, z, z, sel, expand)


def _win_kernel(q_ref, *refs, tq, nprev):
    k_refs = refs[:nprev + 1]
    v_refs = refs[nprev + 1:2 * nprev + 2]
    o_ref, qs_ref = refs[2 * nprev + 2:]
    qi = pl.program_id(1)
    nk = (nprev + 1) * tq
    _stack_group_queries(q_ref, qs_ref, tq)
    k = jnp.concatenate([r[...] for r in k_refs], axis=0)
    v = jnp.concatenate([r[...] for r in v_refs], axis=0)
    t = qi * tq + lax.broadcasted_iota(jnp.int32, (tq, nk), 0)
    kpos = (qi - nprev) * tq + lax.broadcasted_iota(jnp.int32, (tq, nk), 1)
    dist = t - kpos
    bias = jnp.where((dist >= 0) & (dist < WINDOW) & (kpos >= 0), 0.0, NEG)[None]
    outs = []
    for gg in range(2):
        s = lax.dot_general(qs_ref[gg], k, (((1,), (1,)), ((), ())), preferred_element_type=F32)
        s = s.reshape(NSA_HPG, tq, nk) + bias
        m = jnp.max(s, axis=-1, keepdims=True)
        e = jnp.exp(s - m)
        p = e / jnp.sum(e, axis=-1, keepdims=True)
        outs.append(jnp.dot(p.reshape(NSA_HPG * tq, nk).astype(BF16), v,
                            preferred_element_type=F32))
    _merge_pair_heads(outs[0], outs[1], o_ref, tq)


def _nsa_window(z, tq=256):
    S = z.shape[0]
    nprev = WINDOW // tq

    def kv_spec(off, back):
        return pl.BlockSpec((tq, LANES), lambda p, qi: (jnp.maximum(qi - back, 0), off + p))

    backs = list(range(nprev, -1, -1))
    return pl.pallas_call(
        functools.partial(_win_kernel, tq=tq, nprev=nprev),
        grid=(2, S // tq),
        in_specs=[pl.BlockSpec((tq, 512), lambda p, qi: (qi, C_NQ // 512 + p))]
        + [kv_spec(C_KWIN // LANES, b) for b in backs]
        + [kv_spec(C_VWIN // LANES, b) for b in backs],
        out_specs=pl.BlockSpec((tq, 512), lambda p, qi: (qi, p)),
        out_shape=jax.ShapeDtypeStruct((S, NSA_COLS), BF16),
        scratch_shapes=[pltpu.VMEM((2, NSA_HPG * tq, LANES), BF16)],
        compiler_params=_params(("parallel", "parallel")),
        name="nsa_window",
    )(z, *([z] * (2 * nprev + 2)))


def _memkv_kernel(mem_ref, g_ref, w_ref, o_ref):
    memn = _rms(mem_ref[...], g_ref[...]).astype(BF16)
    o_ref[...] = jnp.dot(memn, w_ref[...], preferred_element_type=F32).astype(o_ref.dtype)


def _memkv(mem, g, w):
    M = mem.shape[0]
    return pl.pallas_call(
        _memkv_kernel,
        out_shape=jax.ShapeDtypeStruct((M, 2 * X_HEADS * X_DH), BF16),
        compiler_params=pltpu.CompilerParams(vmem_limit_bytes=VMEM_LIMIT),
        name="mem_kv",
    )(mem, g, w)


def _mid_kernel(x_ref, oda_ref, oc_ref, os_ref, ow_ref, ng_ref, bg0_ref, bg1_ref, eg_ref,
                wda_ref, wnsa_ref, wout_ref, gx_ref, wxq_ref, mkv_ref, wxo_ref, gf_ref,
                h_ref, xnt_ref):
    ng = ng_ref[...]
    o_nsa = jnp.zeros(oc_ref.shape, F32)
    for c, br in enumerate((oc_ref, os_ref, ow_ref)):
        gate = jax.nn.sigmoid(jnp.dot(ng, eg_ref[c], preferred_element_type=F32))
        o_nsa = o_nsa + gate * br[...].astype(F32)
    y_da = jnp.dot(oda_ref[...], wda_ref[...], preferred_element_type=F32)
    y_nsa = jnp.dot(o_nsa.astype(BF16), wnsa_ref[...], preferred_element_type=F32)
    mix = (jax.nn.sigmoid(bg0_ref[...].astype(F32)) * y_da
           + jax.nn.sigmoid(bg1_ref[...].astype(F32)) * y_nsa)
    h = x_ref[...] + jnp.dot(mix.astype(BF16), wout_ref[...], preferred_element_type=F32)
    hn = _rms(h, gx_ref[...]).astype(BF16)
    q = (jnp.dot(hn, wxq_ref[...], preferred_element_type=F32) * (X_DH ** -0.5)).astype(BF16)
    outs = []
    for hd in range(X_HEADS):
        k = mkv_ref[:, hd * X_DH:(hd + 1) * X_DH]
        v = mkv_ref[:, (X_HEADS + hd) * X_DH:(X_HEADS + hd + 1) * X_DH]
        s = lax.dot_general(q[:, hd * X_DH:(hd + 1) * X_DH], k, (((1,), (1,)), ((), ())),
                            preferred_element_type=F32)
        e = jnp.exp(s - jnp.max(s, axis=-1, keepdims=True))
        p = (e / jnp.sum(e, axis=-1, keepdims=True)).astype(BF16)
        outs.append(jnp.dot(p, v, preferred_element_type=F32))
    o = jnp.concatenate(outs, axis=1).astype(BF16)
    h = h + jnp.dot(o, wxo_ref[...], preferred_element_type=F32)
    h_ref[...] = h
    xnt_ref[...] = _rms(h, gf_ref[...]).T.astype(BF16)


def _mid(x, z, o_da, o_c, o_s, o_w, eg, wda, wnsa, wout, gx, wxq, mkv, wxo, gf, tm=256):
    S = x.shape[0]
    M = mkv.shape[0]
    row = lambda w: pl.BlockSpec((tm, w), lambda i: (i, 0))
    return pl.pallas_call(
        _mid_kernel,
        grid=(S // tm,),
        in_specs=[
            row(D_MODEL), row(1024), row(NSA_COLS), row(NSA_COLS), row(NSA_COLS),
            pl.BlockSpec((tm, LANES), lambda i: (i, C_NGB // LANES)),
            pl.BlockSpec((tm, D_MODEL), lambda i: (i, C_BG // D_MODEL)),
            pl.BlockSpec((tm, D_MODEL), lambda i: (i, C_BG // D_MODEL + 1)),
            _const_spec((3, LANES, NSA_COLS)),
            _const_spec((1024, D_MODEL)), _const_spec((NSA_COLS, D_MODEL)),
            _const_spec((D_MODEL, D_MODEL)),
            _const_spec((1, D_MODEL)), _const_spec((D_MODEL, X_HEADS * X_DH)),
            _const_spec((M, 2 * X_HEADS * X_DH)), _const_spec((X_HEADS * X_DH, D_MODEL)),
            _const_spec((1, D_MODEL)),
        ],
        out_specs=[
            pl.BlockSpec((tm, D_MODEL), lambda i: (i, 0)),
            pl.BlockSpec((D_MODEL, tm), lambda i: (0, i)),
        ],
        out_shape=[
            jax.ShapeDtypeStruct((S, D_MODEL), F32),
            jax.ShapeDtypeStruct((D_MODEL, S), BF16),
        ],
        compiler_params=_params(("parallel",)),
        name="merge_xattn",
    )(x, o_da, o_c, o_s, o_w, z, z, z, eg, wda, wnsa, wout, gx, wxq, mkv, wxo, gf)


def _gate_expand():
    e = np.zeros((3, LANES, NSA_COLS), np.float32)
    for c in range(3):
        for g in range(NSA_GROUPS):
            for h in range(NSA_HPG):
                col = ((g // 2) * NSA_HPG + h) * LANES + (g % 2) * NSA_DK
                e[c, NG_LANE0 + g * 12 + h * 3 + c, col:col + NSA_DK] = 1.0
    return jnp.asarray(e, BF16)


def _topk_rows(x, k):
    vals = []
    for _ in range(k):
        m = jnp.max(x, axis=0, keepdims=True)
        vals.append(m)
        x = jnp.where(x == m, -3e38, x)
    return vals


def _peer_pre_kernel(xnt_ref, wqt_ref, k1_ref, k2_ref, t1_ref, s2_ref, e1_ref, e2_ref):
    half = PEER_DQ // 2
    qt = jnp.dot(wqt_ref[...], xnt_ref[...], preferred_element_type=F32).astype(BF16)
    for h in range(PEER_HEADS):
        s1 = jnp.dot(k1_ref[...], qt[h * PEER_DQ:h * PEER_DQ + half], preferred_element_type=F32)
        s2 = jnp.dot(k2_ref[...], qt[h * PEER_DQ + half:(h + 1) * PEER_DQ],
                     preferred_element_type=F32)
        v1 = _topk_rows(s1, PEER_TOPK + 1)
        v2r = _topk_rows(s2, PEER_TOPK + 1)
        v2 = jnp.concatenate(v2r[:PEER_TOPK], axis=0)
        cand = jnp.concatenate([a + v2 for a in v1[:PEER_TOPK]], axis=0)
        top = _topk_rows(cand, PEER_TOPK + 1)
        nxt = jnp.maximum(top[PEER_TOPK], jnp.maximum(v1[PEER_TOPK] + v2r[0], v1[0] + v2r[PEER_TOPK]))
        thr = 0.5 * (top[PEER_TOPK - 1] + nxt)
        zsum = jnp.sum(jnp.where(cand > thr, jnp.exp(cand - top[0]), 0.0), axis=0, keepdims=True)
        t1_ref[h] = thr - s1
        s2_ref[h] = s2
        e1_ref[h] = jnp.exp(s1 - v1[0])
        e2_ref[h] = jnp.exp(s2 - v2[0:1]) / zsum


def _peer_pre(xnt, wqt, k1, k2, tt=512):
    S = xnt.shape[1]
    big = lambda: pl.BlockSpec((PEER_HEADS, PEER_NKEYS, tt), lambda i: (0, 0, i))
    sds = jax.ShapeDtypeStruct((PEER_HEADS, PEER_NKEYS, S), F32)
    return pl.pallas_call(
        _peer_pre_kernel,
        grid=(S // tt,),
        in_specs=[
            pl.BlockSpec((D_MODEL, tt), lambda i: (0, i)),
            _const_spec((PEER_HEADS * PEER_DQ, D_MODEL)),
            _const_spec((PEER_NKEYS, PEER_DQ // 2)),
            _const_spec((PEER_NKEYS, PEER_DQ // 2)),
        ],
        out_specs=[big(), big(), big(), big()],
        out_shape=[sds, sds, sds, sds],
        compiler_params=_params(("parallel",)),
        name="peer_scores",
    )(xnt, wqt, k1, k2)


PEER_JB = 32


def _peer_kernel(xnt_ref, u_ref, vt_ref, t1_ref, s2_ref, e1_ref, e2_ref, o_ref, a_ref, wg_ref,
                 wgp_ref, *, ic):
    e = pl.program_id(1)
    ne = pl.num_programs(1) - 1
    tt = xnt_ref.shape[1]

    @pl.when(e == 0)
    def _():
        o_ref[...] = jnp.zeros(o_ref.shape, F32)
        wg_ref[...] = jnp.zeros(wg_ref.shape, BF16)

    nblk = PEER_NKEYS // PEER_JB

    def apply_previous():
        wgp_ref[...] = wg_ref[...]
        o_ref[...] += jnp.dot(vt_ref[...], wgp_ref[...], preferred_element_type=F32)

    def build_gates(ii):
        assert 8 % ic == 0
        grp = pl.ds(pl.multiple_of((e * ic) // 8 * 8, 8), 8)
        sub = (e * ic) % 8

        def key_row(ref, h, cols):
            rows8 = ref[h, grp, cols]
            row = rows8[ii:ii + 1]
            for k in range(1, 8 // ic):
                row = jnp.where(sub == k * ic, rows8[k * ic + ii:k * ic + ii + 1], row)
            return row

        for c in range(tt // LANES):
            cols = slice(c * LANES, (c + 1) * LANES)
            taus = [key_row(t1_ref, h, cols) for h in range(PEER_HEADS)]
            e1s = [key_row(e1_ref, h, cols) for h in range(PEER_HEADS)]
            for jb in range(nblk):
                j = slice(jb * PEER_JB, (jb + 1) * PEER_JB)
                rows = slice(ii * PEER_NKEYS + jb * PEER_JB, ii * PEER_NKEYS + (jb + 1) * PEER_JB)
                w = jnp.zeros((PEER_JB, LANES), F32)
                for h in range(PEER_HEADS):
                    w = w + jnp.where(s2_ref[h, j, cols] > taus[h],
                                      e2_ref[h, j, cols] * e1s[h], 0.0)
                wg_ref[rows, cols] = (_gelu(a_ref[rows, cols]) * w).astype(BF16)

    @pl.when(e < ne)
    def _():
        a_ref[...] = jnp.dot(u_ref[...], xnt_ref[...], preferred_element_type=F32)
        apply_previous()
        for ii in range(ic):
            build_gates(ii)

    @pl.when(e == ne)
    def _():
        apply_previous()


def _peer(xnt, u, vt, t1, s2, e1, e2, tt=512, ic=4):
    S = xnt.shape[1]
    ec = ic * PEER_NKEYS
    ne = (PEER_NKEYS * PEER_NKEYS) // ec
    tok = pl.BlockSpec((PEER_HEADS, PEER_NKEYS, tt), lambda i, e: (0, 0, i))
    return pl.pallas_call(
        functools.partial(_peer_kernel, ic=ic),
        grid=(S // tt, ne + 1),
        in_specs=[
            pl.BlockSpec((D_MODEL, tt), lambda i, e: (0, i)),
            pl.BlockSpec((ec, D_MODEL), lambda i, e: (jnp.minimum(e, ne - 1), 0)),
            pl.BlockSpec((D_MODEL, ec), lambda i, e: (0, jnp.maximum(e - 1, 0))),
            tok, tok, tok, tok,
        ],
        out_specs=pl.BlockSpec((D_MODEL, tt), lambda i, e: (0, i)),
        out_shape=jax.ShapeDtypeStruct((D_MODEL, S), F32),
        scratch_shapes=[pltpu.VMEM((ec, tt), F32), pltpu.VMEM((ec, tt), BF16),
                        pltpu.VMEM((ec, tt), BF16)],
        compiler_params=_params(("parallel", "arbitrary")),
        name="peer_experts",
    )(xnt, u, vt, t1, s2, e1, e2)


def _final_kernel(h_ref, yt_ref, g_ref, o_ref):
    o_ref[...] = _rms(h_ref[...] + yt_ref[...].T, g_ref[...])


def _final(h, yt, g, tm=256):
    S = h.shape[0]
    return pl.pallas_call(
        _final_kernel,
        grid=(S // tm,),
        in_specs=[
            pl.BlockSpec((tm, D_MODEL), lambda i: (i, 0)),
            pl.BlockSpec((D_MODEL, tm), lambda i: (0, i)),
            pl.BlockSpec((1, D_MODEL), lambda i: (0, 0)),
        ],
        out_specs=pl.BlockSpec((tm, D_MODEL), lambda i: (i, 0)),
        out_shape=jax.ShapeDtypeStruct((S, D_MODEL), F32),
        compiler_params=_params(("parallel",)),
        name="final_norm",
    )(h, yt, g)


def kernel(x, mem, positions, norm_mix_g, w_in, da_lambda_q1, da_lambda_k1, da_lambda_q2, da_lambda_k2, da_subln_g, nsa_cmp_pos, nsa_cmpk_w1, nsa_cmpk_w2, nsa_cmpv_w1, nsa_cmpv_w2, w_branch_da, w_branch_nsa, w_out, norm_x_g, norm_mem_g, w_xq, w_xkv, w_xo, norm_ffn_g, peer_wq, peer_keys1, peer_keys2, peer_u, peer_v, norm_final_g):
    B, S, D = x.shape
    assert B == 1 and D == D_MODEL and norm_mix_g.shape[0] == 1
    l = 0
    lambda_init = 0.8 - 0.6 * math.exp(-0.3 * l)
    lam = (jnp.exp(jnp.sum((da_lambda_q1[l] * da_lambda_k1[l]).astype(F32)))
           - jnp.exp(jnp.sum((da_lambda_q2[l] * da_lambda_k2[l]).astype(F32))) + lambda_init)
    lam_row = jnp.full((1, LANES), lam, F32)

    z = _inproj(x[0], norm_mix_g[l][None], _pack_w_in(w_in[l]), *_rope_tables(positions[0]))
    o_da = _diff_attention(z, lam_row, da_subln_g[l][None], lambda_init)
    kvc = _compress(z, nsa_cmp_pos[l], jnp.stack([nsa_cmpk_w1[l], nsa_cmpv_w1[l]]),
                    jnp.stack([nsa_cmpk_w2[l], nsa_cmpv_w2[l]]))
    o_c, sel = _nsa_compressed(z, kvc)
    o_s = _nsa_selected(z, sel)
    o_w = _nsa_window(z)

    wnsa = w_branch_nsa[l].reshape(2, 2, NSA_HPG, NSA_DK, D).transpose(0, 2, 1, 3, 4)
    wnsa = wnsa.reshape(NSA_COLS, D).astype(BF16)
    mkv = _memkv(mem[0], norm_mem_g[l][None], w_xkv[l].astype(BF16))
    h, xnt = _mid(x[0], z, o_da, o_c, o_s, o_w, _gate_expand(), w_branch_da[l].astype(BF16), wnsa,
                  w_out[l].astype(BF16), norm_x_g[l][None], w_xq[l].astype(BF16), mkv,
                  w_xo[l].astype(BF16), norm_ffn_g[l][None])

    t1, s2, e1, e2 = _peer_pre(xnt, peer_wq[l].T.astype(BF16), peer_keys1[l].astype(BF16),
                               peer_keys2[l].astype(BF16))
    yt = _peer(xnt, peer_u[l].astype(BF16), peer_v[l].T.astype(BF16), t1, s2, e1, e2)
    return _final(h, yt, norm_final_g[None])[None]
```

```python
import functools
import math

import jax
import jax.numpy as jnp
import numpy as np
from jax import lax
from jax.experimental import pallas as pl
from jax.experimental.pallas import tpu as pltpu

F32 = jnp.float32
BF16 = jnp.bfloat16

D_MODEL = 2048
EPS = 1e-6
NEG = -1e30
ROPE_THETA = 500000.0
DA_HEADS = 8
DA_DK = 64
DA_DV = 128
NSA_GROUPS = 4
NSA_HPG = 4
NSA_DK = 64
CMP_LEN = 32
CMP_STRIDE = 16
CMP_HIDDEN = 128
SEL_LEN = 64
SEL_TOPK = 16
WINDOW = 512
FORCE_BONUS = 100.0
ROPE_DIM = 16
X_HEADS = 4
X_DH = 128
PEER_HEADS = 8
PEER_NKEYS = 128
PEER_DQ = 256
PEER_TOPK = 16

LANES = 128
VMEM_LIMIT = 56 * 1024 * 1024

C_ZQ = 0
C_ZK = 1024
C_NQ = 2048
C_KCMP = 3072
C_KSEL = 3328
C_KWIN = 3584
C_NGB = 3840
NG_LANE0 = 16
N_ROPE_COLS = 4096
C_ZV = 4096
C_VCMP = 5120
C_VSEL = 5376
C_VWIN = 5632
C_BG = 6144
N_COLS = 10240
TN_IN = 1024
NSA_COLS = 1024


def _params(sem):
    return pltpu.CompilerParams(dimension_semantics=sem, vmem_limit_bytes=VMEM_LIMIT)


def _const_spec(shape):
    n = len(shape)
    return pl.BlockSpec(shape, lambda *_: (0,) * n, pipeline_mode=pl.Buffered(1))


def _gelu(x):
    return 0.5 * x * (1.0 + lax.erf(x * (2.0 ** -0.5)))


def _rms(x, g):
    ms = jnp.mean(x * x, axis=-1, keepdims=True)
    return x * lax.rsqrt(ms + EPS) * g


def _inproj_kernel(x_ref, g_ref, w_ref, c_ref, sp_ref, sm_ref, o_ref, xn_ref):
    j = pl.program_id(1)

    @pl.when(j == 0)
    def _():
        xn_ref[...] = _rms(x_ref[...], g_ref[...]).astype(BF16)

    z = jnp.dot(xn_ref[...], w_ref[...], preferred_element_type=F32)

    @pl.when(j < N_ROPE_COLS // TN_IN)
    def _():
        c = c_ref[...]
        sp = sp_ref[...]
        sm = sm_ref[...]
        for k in range(TN_IN // LANES):
            zc = z[:, k * LANES:(k + 1) * LANES]
            r = zc * c + pltpu.roll(zc, 8, 1) * sp + pltpu.roll(zc, LANES - 8, 1) * sm
            o_ref[:, k * LANES:(k + 1) * LANES] = r.astype(o_ref.dtype)

    @pl.when(j >= N_ROPE_COLS // TN_IN)
    def _():
        o_ref[...] = z.astype(o_ref.dtype)


def _inproj(x, g, w, rope_c, rope_sp, rope_sm, tm=512):
    S = x.shape[0]
    return pl.pallas_call(
        _inproj_kernel,
        grid=(S // tm, N_COLS // TN_IN),
        in_specs=[
            pl.BlockSpec((tm, D_MODEL), lambda i, j: (i, 0)),
            pl.BlockSpec((1, D_MODEL), lambda i, j: (0, 0)),
            pl.BlockSpec((D_MODEL, TN_IN), lambda i, j: (0, j)),
            pl.BlockSpec((tm, LANES), lambda i, j: (i, 0)),
            pl.BlockSpec((tm, LANES), lambda i, j: (i, 0)),
            pl.BlockSpec((tm, LANES), lambda i, j: (i, 0)),
        ],
        out_specs=pl.BlockSpec((tm, TN_IN), lambda i, j: (i, j)),
        out_shape=jax.ShapeDtypeStruct((S, N_COLS), BF16),
        scratch_shapes=[pltpu.VMEM((tm, D_MODEL), BF16)],
        compiler_params=_params(("parallel", "arbitrary")),
        name="inproj",
    )(x, g, w, rope_c, rope_sp, rope_sm)


def _pack_w_in(w_in):
    zq, zk, zv, nq, nkv, ng, bg = jnp.split(
        w_in.astype(BF16), np.cumsum([1024, 1024, 1024, 1024, 1536, 48]).tolist(), axis=1)
    D = w_in.shape[0]
    nq = nq.reshape(D, 2, 2, NSA_HPG, NSA_DK).transpose(0, 1, 3, 2, 4).reshape(D, 1024)
    nkv = nkv.reshape(D, 6, 256)
    ngb = jnp.concatenate([jnp.zeros((D, NG_LANE0), BF16), ng,
                           jnp.zeros((D, 256 - NG_LANE0 - 48), BF16)], axis=1)
    pad = jnp.zeros((D, C_BG - C_VWIN - 256), BF16)
    return jnp.concatenate([zq, zk, nq, nkv[:, 0], nkv[:, 2], nkv[:, 4], ngb,
                            zv, nkv[:, 1], nkv[:, 3], nkv[:, 5], pad, bg], axis=1)


def _rope_tables(positions):
    inv = ROPE_THETA ** (-jnp.arange(0, ROPE_DIM, 2, dtype=F32) / ROPE_DIM)
    ang = positions.astype(F32)[:, None] * inv
    cos, sin = jnp.cos(ang), jnp.sin(ang)
    S = positions.shape[0]
    one = jnp.ones((S, 48), F32)
    zero = jnp.zeros((S, 48), F32)
    z8 = jnp.zeros((S, 8), F32)
    c64 = jnp.concatenate([cos, cos, one], axis=1)
    sp64 = jnp.concatenate([z8, sin, zero], axis=1)
    sm64 = jnp.concatenate([-sin, z8, zero], axis=1)
    return (jnp.tile(c64, (1, 2)), jnp.tile(sp64, (1, 2)), jnp.tile(sm64, (1, 2)))


def _online_softmax_step(s, v, m_ref, l_ref, acc_ref):
    m_prev = m_ref[...]
    m_new = jnp.maximum(m_prev, jnp.max(s, axis=-1, keepdims=True))
    alpha = jnp.exp(m_prev - m_new)
    p = jnp.exp(s - m_new[:, 0:1])
    l_ref[...] = alpha * l_ref[...] + jnp.sum(p, axis=-1, keepdims=True)
    acc_ref[...] = alpha * acc_ref[...] + jnp.dot(p.astype(BF16), v, preferred_element_type=F32)
    m_ref[...] = m_new


DA_SPLIT = 2
DA_WIDE = 4


def _da_kernel(lam_ref, q_ref, k_ref, v_ref, g_ref, o_ref, qs_ref, m_ref, l_ref, acc_ref,
               *, tq, out_scale):
    qi = pl.program_id(1)
    q = q_ref[...].astype(F32) * (DA_DK ** -0.5)
    lane = lax.broadcasted_iota(jnp.int32, q.shape, 1)
    qs_ref[0:tq, :] = jnp.where(lane < DA_DK, q, 0.0).astype(BF16)
    qs_ref[tq:2 * tq, :] = jnp.where(lane >= DA_DK, q, 0.0).astype(BF16)
    m_ref[...] = jnp.full(m_ref.shape, NEG, F32)
    l_ref[...] = jnp.zeros(l_ref.shape, F32)
    acc_ref[...] = jnp.zeros(acc_ref.shape, F32)

    rc = tq // DA_SPLIT

    def tile(off, width, masked):
        off = pl.multiple_of(off, tq)
        k = k_ref[pl.ds(off, width), :]
        v = v_ref[pl.ds(off, width), :]
        for c in range(2 * DA_SPLIT):
            r = slice(c * rc, (c + 1) * rc)
            s = lax.dot_general(qs_ref[r, :], k, (((1,), (1,)), ((), ())),
                                preferred_element_type=F32)
            if masked:
                row = (c % DA_SPLIT) * rc + lax.broadcasted_iota(jnp.int32, (rc, width), 0)
                col = lax.broadcasted_iota(jnp.int32, (rc, width), 1)
                s = jnp.where(col <= row + (width - tq), s, NEG)
            _online_softmax_step(s, v, m_ref.at[r], l_ref.at[r], acc_ref.at[r])

    def wide(kw, carry):
        tile(kw * (DA_WIDE * tq), DA_WIDE * tq, False)
        return carry

    nw = qi // DA_WIDE
    lax.fori_loop(0, nw, wide, 0)
    rem = qi - nw * DA_WIDE
    for r_ in range(DA_WIDE):
        @pl.when(rem == r_)
        def _(r_=r_):
            tile(nw * (DA_WIDE * tq), (r_ + 1) * tq, True)

    o = acc_ref[...] / l_ref[...]
    d = o[0:tq] - lam_ref[...] * o[tq:2 * tq]
    o_ref[...] = (_rms(d, g_ref[...]) * out_scale).astype(o_ref.dtype)


def _diff_attention(z, lam_row, subln_g, lambda_init, tq=512):
    S = z.shape[0]
    return pl.pallas_call(
        functools.partial(_da_kernel, tq=tq, out_scale=1.0 - lambda_init),
        grid=(DA_HEADS, S // tq),
        in_specs=[
            pl.BlockSpec((1, LANES), lambda h, qi: (0, 0)),
            pl.BlockSpec((tq, LANES), lambda h, qi: (qi, C_ZQ // LANES + h)),
            pl.BlockSpec((S, LANES), lambda h, qi: (0, C_ZK // LANES + h)),
            pl.BlockSpec((S, LANES), lambda h, qi: (0, C_ZV // LANES + h)),
            pl.BlockSpec((1, LANES), lambda h, qi: (0, 0)),
        ],
        out_specs=pl.BlockSpec((tq, LANES), lambda h, qi: (qi, h)),
        out_shape=jax.ShapeDtypeStruct((S, DA_HEADS * DA_DV), BF16),
        scratch_shapes=[
            pltpu.VMEM((2 * tq, LANES), BF16),
            pltpu.VMEM((2 * tq, LANES), F32),
            pltpu.VMEM((2 * tq, LANES), F32),
            pltpu.VMEM((2 * tq, LANES), F32),
        ],
        compiler_params=_params(("parallel", "arbitrary")),
        name="diff_attn",
    )(lam_row, z, z, z, subln_g)


def _compress_kernel(x_ref, pos_ref, w1_ref, w2_ref, o_ref):
    n = x_ref.shape[3]
    w1a = w1_ref[0, 0:CMP_STRIDE * NSA_DK, :]
    w1b = w1_ref[0, CMP_STRIDE * NSA_DK:, :]
    c = jnp.dot(pos_ref[...], w1_ref[0], preferred_element_type=F32)[0:1]
    outs = []
    for gg in range(2):
        xs = x_ref[0, 0, gg]
        y1 = jnp.dot(xs, w1a, preferred_element_type=F32)
        y2 = jnp.dot(xs, w1b, preferred_element_type=F32)
        hid = _gelu(y1 + pltpu.roll(y2, n - 1, 0) + c)
        outs.append(jnp.dot(hid.astype(BF16), w2_ref[0], preferred_element_type=F32))
    o_ref[0] = jnp.concatenate(outs, axis=1).astype(o_ref.dtype)


def _compress(z, pos, w1, w2):
    S = z.shape[0]
    n = S // CMP_STRIDE

    def blocks(c0):
        a = z[:, c0:c0 + 256].reshape(n, CMP_STRIDE, 2, 2, NSA_DK)
        return a.transpose(2, 3, 0, 1, 4).reshape(2, 2, n, CMP_STRIDE * NSA_DK)

    xs = jnp.stack([blocks(C_KCMP), blocks(C_VCMP)])
    pos8 = jnp.broadcast_to(pos.reshape(1, CMP_LEN * NSA_DK), (8, CMP_LEN * NSA_DK)).astype(BF16)
    return pl.pallas_call(
        _compress_kernel,
        grid=(2, 2),
        in_specs=[
            pl.BlockSpec((1, 1, 2, n, CMP_STRIDE * NSA_DK), lambda c, p: (c, p, 0, 0, 0)),
            pl.BlockSpec((8, CMP_LEN * NSA_DK), lambda c, p: (0, 0)),
            pl.BlockSpec((1, CMP_LEN * NSA_DK, CMP_HIDDEN), lambda c, p: (c, 0, 0)),
            pl.BlockSpec((1, CMP_HIDDEN, NSA_DK), lambda c, p: (c, 0, 0)),
        ],
        out_specs=pl.BlockSpec((1, n, LANES), lambda c, p: (c, 0, p)),
        out_shape=jax.ShapeDtypeStruct((2, n, 256), BF16),
        compiler_params=_params(("parallel", "parallel")),
        name="nsa_compress",
    )(xs, pos8, w1.astype(BF16), w2.astype(BF16))


def _stack_group_queries(q_ref, qs_ref, tq):
    lane = lax.broadcasted_iota(jnp.int32, (tq, LANES), 1)
    for h in range(NSA_HPG):
        q = q_ref[:, h * LANES:(h + 1) * LANES].astype(F32) * (NSA_DK ** -0.5)
        qs_ref[0, h * tq:(h + 1) * tq, :] = jnp.where(lane < NSA_DK, q, 0.0).astype(BF16)
        qs_ref[1, h * tq:(h + 1) * tq, :] = jnp.where(lane >= NSA_DK, q, 0.0).astype(BF16)


def _merge_pair_heads(o0, o1, o_ref, tq):
    lane = lax.broadcasted_iota(jnp.int32, (tq, LANES), 1)
    for h in range(NSA_HPG):
        o_ref[:, h * LANES:(h + 1) * LANES] = jnp.where(
            lane < NSA_DK, o0[h * tq:(h + 1) * tq], o1[h * tq:(h + 1) * tq]).astype(o_ref.dtype)


def _cmp_kernel(q_ref, kc_ref, vc_ref, pool_ref, o_ref, sel_ref, qs_ref, *, tq):
    qi = pl.program_id(1)
    n = kc_ref.shape[1]
    _stack_group_queries(q_ref, qs_ref, tq)
    t = qi * tq + lax.broadcasted_iota(jnp.int32, (tq, n), 0)
    cmp_end = lax.broadcasted_iota(jnp.int32, (tq, n), 1) * CMP_STRIDE + (CMP_LEN - 1)
    vis = (cmp_end <= t)[None]
    tb = qi * tq + lax.broadcasted_iota(jnp.int32, (tq, LANES), 0)
    blk = lax.broadcasted_iota(jnp.int32, (tq, LANES), 1)
    elig = blk * SEL_LEN <= tb
    cur = tb // SEL_LEN
    forced = (blk == 0) | (blk == cur) | (blk == cur - 1)
    outs, scores = [], []
    for gg in range(2):
        s = lax.dot_general(qs_ref[gg], kc_ref[0], (((1,), (1,)), ((), ())),
                            preferred_element_type=F32).reshape(NSA_HPG, tq, n)
        s = jnp.where(vis, s, NEG)
        m = jnp.max(s, axis=-1, keepdims=True)
        e = jnp.where(vis, jnp.exp(s - m), 0.0)
        den = jnp.sum(e, axis=-1, keepdims=True)
        p = e / jnp.where(den == 0.0, 1.0, den)
        outs.append(jnp.dot(p.reshape(NSA_HPG * tq, n).astype(BF16), vc_ref[0],
                            preferred_element_type=F32))
        imp = jnp.sum(p, axis=0)
        hi = imp.astype(BF16)
        r1 = imp - hi.astype(F32)
        mid = r1.astype(BF16)
        lo = (r1 - mid.astype(F32)).astype(BF16)
        pooled = (jnp.dot(hi, pool_ref[...], preferred_element_type=F32)
                  + jnp.dot(mid, pool_ref[...], preferred_element_type=F32)
                  + jnp.dot(lo, pool_ref[...], preferred_element_type=F32))
        scores.append(jnp.where(elig, pooled + jnp.where(forced, FORCE_BONUS, 0.0), -1.0))
    sels = [jnp.zeros((tq, LANES), F32), jnp.zeros((tq, LANES), F32)]
    for _ in range(SEL_TOPK):
        for gg in range(2):
            mx = jnp.max(scores[gg], axis=-1, keepdims=True)
            first = jnp.min(jnp.where(scores[gg] == mx, blk, LANES), axis=-1, keepdims=True)
            hit = blk == first
            sels[gg] = jnp.where(hit & elig, 1.0, sels[gg])
            scores[gg] = jnp.where(hit, -3e38, scores[gg])
    for gg in range(2):
        sel_ref[:, gg * LANES:(gg + 1) * LANES] = sels[gg].astype(sel_ref.dtype)
    _merge_pair_heads(outs[0], outs[1], o_ref, tq)


def _nsa_compressed(z, kvc, tq=512):
    S = z.shape[0]
    n = S // CMP_STRIDE
    pool = (np.arange(n)[:, None] // (SEL_LEN // CMP_STRIDE) == np.arange(LANES)[None, :])
    pool = jnp.asarray(pool, BF16)
    return pl.pallas_call(
        functools.partial(_cmp_kernel, tq=tq),
        grid=(2, S // tq),
        in_specs=[
            pl.BlockSpec((tq, 512), lambda p, qi: (qi, C_NQ // 512 + p)),
            pl.BlockSpec((1, n, LANES), lambda p, qi: (0, 0, p)),
            pl.BlockSpec((1, n, LANES), lambda p, qi: (1, 0, p)),
            pl.BlockSpec((n, LANES), lambda p, qi: (0, 0)),
        ],
        out_specs=[
            pl.BlockSpec((tq, 512), lambda p, qi: (qi, p)),
            pl.BlockSpec((tq, 256), lambda p, qi: (qi, p)),
        ],
        out_shape=[
            jax.ShapeDtypeStruct((S, NSA_COLS), BF16),
            jax.ShapeDtypeStruct((S, 512), BF16),
        ],
        scratch_shapes=[pltpu.VMEM((2, NSA_HPG * tq, LANES), BF16)],
        compiler_params=_params(("parallel", "parallel")),
        name="nsa_cmp_topk",
    )(z, kvc, kvc, pool)


def _sel_kernel(q_ref, k_ref, v_ref, sel_ref, ex_ref, o_ref, qs_ref, m_ref, l_ref, acc_ref,
                *, tq, tk):
    qi = pl.program_id(1)
    _stack_group_queries(q_ref, qs_ref, tq)
    m_ref[...] = jnp.full(m_ref.shape, NEG, F32)
    l_ref[...] = jnp.zeros(l_ref.shape, F32)
    acc_ref[...] = jnp.zeros(acc_ref.shape, F32)
    row = qi * tq + lax.broadcasted_iota(jnp.int32, (tq, tk), 0)
    col = lax.broadcasted_iota(jnp.int32, (tq, tk), 1)

    def body(ki, carry):
        off = pl.multiple_of(ki * tk, tk)
        k = k_ref[pl.ds(off, tk), :]
        v = v_ref[pl.ds(off, tk), :]
        causal = off + col <= row
        for gg in range(2):
            hit = jnp.dot(sel_ref[:, gg * LANES:(gg + 1) * LANES], ex_ref[ki],
                          preferred_element_type=F32)
            bias = jnp.where(causal & (hit > 0.5), 0.0, NEG)
            s = lax.dot_general(qs_ref[gg], k, (((1,), (1,)), ((), ())),
                                preferred_element_type=F32)
            s = (s.reshape(NSA_HPG, tq, tk) + bias[None]).reshape(NSA_HPG * tq, tk)
            _online_softmax_step(s, v, m_ref.at[gg], l_ref.at[gg], acc_ref.at[gg])
        return carry

    lax.fori_loop(0, (qi * tq + tq - 1) // tk + 1, body, 0)
    _merge_pair_heads(acc_ref[0] / l_ref[0], acc_ref[1] / l_ref[1], o_ref, tq)


def _nsa_selected(z, sel, tq=256, tk=1024):
    S = z.shape[0]
    tk = min(tk, S)
    expand = np.arange(LANES)[:, None] == np.arange(S)[None, :] // SEL_LEN
    expand = jnp.asarray(expand.reshape(LANES, S // tk, tk).transpose(1, 0, 2), BF16)
    return pl.pallas_call(
        functools.partial(_sel_kernel, tq=tq, tk=tk),
        grid=(2, S // tq),
        in_specs=[
            pl.BlockSpec((tq, 512), lambda p, qi: (qi, C_NQ // 512 + p)),
            pl.BlockSpec((S, LANES), lambda p, qi: (0, C_KSEL // LANES + p)),
            pl.BlockSpec((S, LANES), lambda p, qi: (0, C_VSEL // LANES + p)),
            pl.BlockSpec((tq, 256), lambda p, qi: (qi, p)),
            pl.BlockSpec((S // tk, LANES, tk), lambda p, qi: (0, 0, 0)),
        ],
        out_specs=pl.BlockSpec((tq, 512), lambda p, qi: (qi, p)),
        out_shape=jax.ShapeDtypeStruct((S, NSA_COLS), BF16),
        scratch_shapes=[
            pltpu.VMEM((2, NSA_HPG * tq, LANES), BF16),
            pltpu.VMEM((2, NSA_HPG * tq, LANES), F32),
            pltpu.VMEM((2, NSA_HPG * tq, LANES), F32),
            pltpu.VMEM((2, NSA_HPG * tq, LANES), F32),
        ],
        compiler_params=_params(("parallel", "arbitrary")),
        name="nsa_selected",
    )(z, z, z, sel, expand)


def _win_kernel(q_ref, *refs, tq, nprev):
    k_refs = refs[:nprev + 1]
    v_refs = refs[nprev + 1:2 * nprev + 2]
    o_ref, qs_ref = refs[2 * nprev + 2:]
    qi = pl.program_id(1)
    nk = (nprev + 1) * tq
    _stack_group_queries(q_ref, qs_ref, tq)
    k = jnp.concatenate([r[...] for r in k_refs], axis=0)
    v = jnp.concatenate([r[...] for r in v_refs], axis=0)
    t = qi * tq + lax.broadcasted_iota(jnp.int32, (tq, nk), 0)
    kpos = (qi - nprev) * tq + lax.broadcasted_iota(jnp.int32, (tq, nk), 1)
    dist = t - kpos
    bias = jnp.where((dist >= 0) & (dist < WINDOW) & (kpos >= 0), 0.0, NEG)[None]
    outs = []
    for gg in range(2):
        s = lax.dot_general(qs_ref[gg], k, (((1,), (1,)), ((), ())), preferred_element_type=F32)
        s = s.reshape(NSA_HPG, tq, nk) + bias
        m = jnp.max(s, axis=-1, keepdims=True)
        e = jnp.exp(s - m)
        p = e / jnp.sum(e, axis=-1, keepdims=True)
        outs.append(jnp.dot(p.reshape(NSA_HPG * tq, nk).astype(BF16), v,
                            preferred_element_type=F32))
    _merge_pair_heads(outs[0], outs[1], o_ref, tq)


def _nsa_window(z, tq=256):
    S = z.shape[0]
    nprev = WINDOW // tq

    def kv_spec(off, back):
        return pl.BlockSpec((tq, LANES), lambda p, qi: (jnp.maximum(qi - back, 0), off + p))

    backs = list(range(nprev, -1, -1))
    return pl.pallas_call(
        functools.partial(_win_kernel, tq=tq, nprev=nprev),
        grid=(2, S // tq),
        in_specs=[pl.BlockSpec((tq, 512), lambda p, qi: (qi, C_NQ // 512 + p))]
        + [kv_spec(C_KWIN // LANES, b) for b in backs]
        + [kv_spec(C_VWIN // LANES, b) for b in backs],
        out_specs=pl.BlockSpec((tq, 512), lambda p, qi: (qi, p)),
        out_shape=jax.ShapeDtypeStruct((S, NSA_COLS), BF16),
        scratch_shapes=[pltpu.VMEM((2, NSA_HPG * tq, LANES), BF16)],
        compiler_params=_params(("parallel", "parallel")),
        name="nsa_window",
    )(z, *([z] * (2 * nprev + 2)))


def _memkv_kernel(mem_ref, g_ref, w_ref, o_ref):
    memn = _rms(mem_ref[...], g_ref[...]).astype(BF16)
    o_ref[...] = jnp.dot(memn, w_ref[...], preferred_element_type=F32).astype(o_ref.dtype)


def _memkv(mem, g, w):
    M = mem.shape[0]
    return pl.pallas_call(
        _memkv_kernel,
        out_shape=jax.ShapeDtypeStruct((M, 2 * X_HEADS * X_DH), BF16),
        compiler_params=pltpu.CompilerParams(vmem_limit_bytes=VMEM_LIMIT),
        name="mem_kv",
    )(mem, g, w)


def _mid_kernel(x_ref, oda_ref, oc_ref, os_ref, ow_ref, ng_ref, bg0_ref, bg1_ref, eg_ref,
                wda_ref, wnsa_ref, wout_ref, gx_ref, wxq_ref, mkv_ref, wxo_ref, gf_ref,
                h_ref, xnt_ref):
    ng = ng_ref[...]
    o_nsa = jnp.zeros(oc_ref.shape, F32)
    for c, br in enumerate((oc_ref, os_ref, ow_ref)):
        gate = jax.nn.sigmoid(jnp.dot(ng, eg_ref[c], preferred_element_type=F32))
        o_nsa = o_nsa + gate * br[...].astype(F32)
    y_da = jnp.dot(oda_ref[...], wda_ref[...], preferred_element_type=F32)
    y_nsa = jnp.dot(o_nsa.astype(BF16), wnsa_ref[...], preferred_element_type=F32)
    mix = (jax.nn.sigmoid(bg0_ref[...].astype(F32)) * y_da
           + jax.nn.sigmoid(bg1_ref[...].astype(F32)) * y_nsa)
    h = x_ref[...] + jnp.dot(mix.astype(BF16), wout_ref[...], preferred_element_type=F32)
    hn = _rms(h, gx_ref[...]).astype(BF16)
    q = (jnp.dot(hn, wxq_ref[...], preferred_element_type=F32) * (X_DH ** -0.5)).astype(BF16)
    outs = []
    for hd in range(X_HEADS):
        k = mkv_ref[:, hd * X_DH:(hd + 1) * X_DH]
        v = mkv_ref[:, (X_HEADS + hd) * X_DH:(X_HEADS + hd + 1) * X_DH]
        s = lax.dot_general(q[:, hd * X_DH:(hd + 1) * X_DH], k, (((1,), (1,)), ((), ())),
                            preferred_element_type=F32)
        e = jnp.exp(s - jnp.max(s, axis=-1, keepdims=True))
        p = (e / jnp.sum(e, axis=-1, keepdims=True)).astype(BF16)
        outs.append(jnp.dot(p, v, preferred_element_type=F32))
    o = jnp.concatenate(outs, axis=1).astype(BF16)
    h = h + jnp.dot(o, wxo_ref[...], preferred_element_type=F32)
    h_ref[...] = h
    xnt_ref[...] = _rms(h, gf_ref[...]).T.astype(BF16)


def _mid(x, z, o_da, o_c, o_s, o_w, eg, wda, wnsa, wout, gx, wxq, mkv, wxo, gf, tm=256):
    S = x.shape[0]
    M = mkv.shape[0]
    row = lambda w: pl.BlockSpec((tm, w), lambda i: (i, 0))
    return pl.pallas_call(
        _mid_kernel,
        grid=(S // tm,),
        in_specs=[
            row(D_MODEL), row(1024), row(NSA_COLS), row(NSA_COLS), row(NSA_COLS),
            pl.BlockSpec((tm, LANES), lambda i: (i, C_NGB // LANES)),
            pl.BlockSpec((tm, D_MODEL), lambda i: (i, C_BG // D_MODEL)),
            pl.BlockSpec((tm, D_MODEL), lambda i: (i, C_BG // D_MODEL + 1)),
            _const_spec((3, LANES, NSA_COLS)),
            _const_spec((1024, D_MODEL)), _const_spec((NSA_COLS, D_MODEL)),
            _const_spec((D_MODEL, D_MODEL)),
            _const_spec((1, D_MODEL)), _const_spec((D_MODEL, X_HEADS * X_DH)),
            _const_spec((M, 2 * X_HEADS * X_DH)), _const_spec((X_HEADS * X_DH, D_MODEL)),
            _const_spec((1, D_MODEL)),
        ],
        out_specs=[
            pl.BlockSpec((tm, D_MODEL), lambda i: (i, 0)),
            pl.BlockSpec((D_MODEL, tm), lambda i: (0, i)),
        ],
        out_shape=[
            jax.ShapeDtypeStruct((S, D_MODEL), F32),
            jax.ShapeDtypeStruct((D_MODEL, S), BF16),
        ],
        compiler_params=_params(("parallel",)),
        name="merge_xattn",
    )(x, o_da, o_c, o_s, o_w, z, z, z, eg, wda, wnsa, wout, gx, wxq, mkv, wxo, gf)


def _gate_expand():
    e = np.zeros((3, LANES, NSA_COLS), np.float32)
    for c in range(3):
        for g in range(NSA_GROUPS):
            for h in range(NSA_HPG):
                col = ((g // 2) * NSA_HPG + h) * LANES + (g % 2) * NSA_DK
                e[c, NG_LANE0 + g * 12 + h * 3 + c, col:col + NSA_DK] = 1.0
    return jnp.asarray(e, BF16)


def _topk_rows(x, k):
    vals = []
    for _ in range(k):
        m = jnp.max(x, axis=0, keepdims=True)
        vals.append(m)
        x = jnp.where(x == m, -3e38, x)
    return vals


def _peer_pre_kernel(xnt_ref, wqt_ref, k1_ref, k2_ref, t1_ref, s2_ref, e1_ref, e2_ref):
    half = PEER_DQ // 2
    qt = jnp.dot(wqt_ref[...], xnt_ref[...], preferred_element_type=F32).astype(BF16)
    for h in range(PEER_HEADS):
        s1 = jnp.dot(k1_ref[...], qt[h * PEER_DQ:h * PEER_DQ + half], preferred_element_type=F32)
        s2 = jnp.dot(k2_ref[...], qt[h * PEER_DQ + half:(h + 1) * PEER_DQ],
                     preferred_element_type=F32)
        v1 = _topk_rows(s1, PEER_TOPK + 1)
        v2r = _topk_rows(s2, PEER_TOPK + 1)
        v2 = jnp.concatenate(v2r[:PEER_TOPK], axis=0)
        cand = jnp.concatenate([a + v2 for a in v1[:PEER_TOPK]], axis=0)
        top = _topk_rows(cand, PEER_TOPK + 1)
        nxt = jnp.maximum(top[PEER_TOPK], jnp.maximum(v1[PEER_TOPK] + v2r[0], v1[0] + v2r[PEER_TOPK]))
        thr = 0.5 * (top[PEER_TOPK - 1] + nxt)
        zsum = jnp.sum(jnp.where(cand > thr, jnp.exp(cand - top[0]), 0.0), axis=0, keepdims=True)
        t1_ref[h] = thr - s1
        s2_ref[h] = s2
        e1_ref[h] = jnp.exp(s1 - v1[0])
        e2_ref[h] = jnp.exp(s2 - v2[0:1]) / zsum


def _peer_pre(xnt, wqt, k1, k2, tt=512):
    S = xnt.shape[1]
    big = lambda: pl.BlockSpec((PEER_HEADS, PEER_NKEYS, tt), lambda i: (0, 0, i))
    sds = jax.ShapeDtypeStruct((PEER_HEADS, PEER_NKEYS, S), F32)
    return pl.pallas_call(
        _peer_pre_kernel,
        grid=(S // tt,),
        in_specs=[
            pl.BlockSpec((D_MODEL, tt), lambda i: (0, i)),
            _const_spec((PEER_HEADS * PEER_DQ, D_MODEL)),
            _const_spec((PEER_NKEYS, PEER_DQ // 2)),
            _const_spec((PEER_NKEYS, PEER_DQ // 2)),
        ],
        out_specs=[big(), big(), big(), big()],
        out_shape=[sds, sds, sds, sds],
        compiler_params=_params(("parallel",)),
        name="peer_scores",
    )(xnt, wqt, k1, k2)


PEER_JB = 32


def _peer_kernel(xnt_ref, u_ref, vt_ref, t1_ref, s2_ref, e1_ref, e2_ref, o_ref, a_ref, wg_ref,
                 wgp_ref, *, ic):
    e = pl.program_id(1)
    ne = pl.num_programs(1) - 1
    tt = xnt_ref.shape[1]

    @pl.when(e == 0)
    def _():
        o_ref[...] = jnp.zeros(o_ref.shape, F32)
        wg_ref[...] = jnp.zeros(wg_ref.shape, BF16)

    nblk = PEER_NKEYS // PEER_JB

    def apply_previous():
        wgp_ref[...] = wg_ref[...]
        o_ref[...] += jnp.dot(vt_ref[...], wgp_ref[...], preferred_element_type=F32)

    def build_gates(ii):
        assert 8 % ic == 0
        grp = pl.ds(pl.multiple_of((e * ic) // 8 * 8, 8), 8)
        sub = (e * ic) % 8

        def key_row(ref, h, cols):
            rows8 = ref[h, grp, cols]
            row = rows8[ii:ii + 1]
            for k in range(1, 8 // ic):
                row = jnp.where(sub == k * ic, rows8[k * ic + ii:k * ic + ii + 1], row)
            return row

        for c in range(tt // LANES):
            cols = slice(c * LANES, (c + 1) * LANES)
            taus = [key_row(t1_ref, h, cols) for h in range(PEER_HEADS)]
            e1s = [key_row(e1_ref, h, cols) for h in range(PEER_HEADS)]
            for jb in range(nblk):
                j = slice(jb * PEER_JB, (jb + 1) * PEER_JB)
                rows = slice(ii * PEER_NKEYS + jb * PEER_JB, ii * PEER_NKEYS + (jb + 1) * PEER_JB)
                w = jnp.zeros((PEER_JB, LANES), F32)
                for h in range(PEER_HEADS):
                    w = w + jnp.where(s2_ref[h, j, cols] > taus[h],
                                      e2_ref[h, j, cols] * e1s[h], 0.0)
                wg_ref[rows, cols] = (_gelu(a_ref[rows, cols]) * w).astype(BF16)

    @pl.when(e < ne)
    def _():
        a_ref[...] = jnp.dot(u_ref[...], xnt_ref[...], preferred_element_type=F32)
        apply_previous()
        for ii in range(ic):
            build_gates(ii)

    @pl.when(e == ne)
    def _():
        apply_previous()


def _peer(xnt, u, vt, t1, s2, e1, e2, tt=512, ic=4):
    S = xnt.shape[1]
    ec = ic * PEER_NKEYS
    ne = (PEER_NKEYS * PEER_NKEYS) // ec
    tok = pl.BlockSpec((PEER_HEADS, PEER_NKEYS, tt), lambda i, e: (0, 0, i))
    return pl.pallas_call(
        functools.partial(_peer_kernel, ic=ic),
        grid=(S // tt, ne + 1),
        in_specs=[
            pl.BlockSpec((D_MODEL, tt), lambda i, e: (0, i)),
            pl.BlockSpec((ec, D_MODEL), lambda i, e: (jnp.minimum(e, ne - 1), 0)),
            pl.BlockSpec((D_MODEL, ec), lambda i, e: (0, jnp.maximum(e - 1, 0))),
            tok, tok, tok, tok,
        ],
        out_specs=pl.BlockSpec((D_MODEL, tt), lambda i, e: (0, i)),
        out_shape=jax.ShapeDtypeStruct((D_MODEL, S), F32),
        scratch_shapes=[pltpu.VMEM((ec, tt), F32), pltpu.VMEM((ec, tt), BF16),
                        pltpu.VMEM((ec, tt), BF16)],
        compiler_params=_params(("parallel", "arbitrary")),
        name="peer_experts",
    )(xnt, u, vt, t1, s2, e1, e2)


def _final_kernel(h_ref, yt_ref, g_ref, o_ref):
    o_ref[...] = _rms(h_ref[...] + yt_ref[...].T, g_ref[...])


def _final(h, yt, g, tm=256):
    S = h.shape[0]
    return pl.pallas_call(
        _final_kernel,
        grid=(S // tm,),
        in_specs=[
            pl.BlockSpec((tm, D_MODEL), lambda i: (i, 0)),
            pl.BlockSpec((D_MODEL, tm), lambda i: (0, i)),
            pl.BlockSpec((1, D_MODEL), lambda i: (0, 0)),
        ],
        out_specs=pl.BlockSpec((tm, D_MODEL), lambda i: (i, 0)),
        out_shape=jax.ShapeDtypeStruct((S, D_MODEL), F32),
        compiler_params=_params(("parallel",)),
        name="final_norm",
    )(h, yt, g)


def kernel(x, mem, positions, norm_mix_g, w_in, da_lambda_q1, da_lambda_k1, da_lambda_q2, da_lambda_k2, da_subln_g, nsa_cmp_pos, nsa_cmpk_w1, nsa_cmpk_w2, nsa_cmpv_w1, nsa_cmpv_w2, w_branch_da, w_branch_nsa, w_out, norm_x_g, norm_mem_g, w_xq, w_xkv, w_xo, norm_ffn_g, peer_wq, peer_keys1, peer_keys2, peer_u, peer_v, norm_final_g):
    B, S, D = x.shape
    assert B == 1 and D == D_MODEL and norm_mix_g.shape[0] == 1
    l = 0
    lambda_init = 0.8 - 0.6 * math.exp(-0.3 * l)
    lam = (jnp.exp(jnp.sum((da_lambda_q1[l] * da_lambda_k1[l]).astype(F32)))
           - jnp.exp(jnp.sum((da_lambda_q2[l] * da_lambda_k2[l]).astype(F32))) + lambda_init)
    lam_row = jnp.full((1, LANES), lam, F32)

    z = _inproj(x[0], norm_mix_g[l][None], _pack_w_in(w_in[l]), *_rope_tables(positions[0]))
    o_da = _diff_attention(z, lam_row, da_subln_g[l][None], lambda_init)
    kvc = _compress(z, nsa_cmp_pos[l], jnp.stack([nsa_cmpk_w1[l], nsa_cmpv_w1[l]]),
                    jnp.stack([nsa_cmpk_w2[l], nsa_cmpv_w2[l]]))
    o_c, sel = _nsa_compressed(z, kvc)
    o_s = _nsa_selected(z, sel)
    o_w = _nsa_window(z)

    wnsa = w_branch_nsa[l].reshape(2, 2, NSA_HPG, NSA_DK, D).transpose(0, 2, 1, 3, 4)
    wnsa = wnsa.reshape(NSA_COLS, D).astype(BF16)
    mkv = _memkv(mem[0], norm_mem_g[l][None], w_xkv[l].astype(BF16))
    h, xnt = _mid(x[0], z, o_da, o_c, o_s, o_w, _gate_expand(), w_branch_da[l].astype(BF16), wnsa,
                  w_out[l].astype(BF16), norm_x_g[l][None], w_xq[l].astype(BF16), mkv,
                  w_xo[l].astype(BF16), norm_ffn_g[l][None])

    t1, s2, e1, e2 = _peer_pre(xnt, peer_wq[l].T.astype(BF16), peer_keys1[l].astype(BF16),
                               peer_keys2[l].astype(BF16))
    yt = _peer(xnt, peer_u[l].astype(BF16), peer_v[l].T.astype(BF16), t1, s2, e1, e2)
    return _final(h, yt, norm_final_g[None])[None]
```
